```python
import jax, jax.numpy as jnp
from jax import lax
import numpy as np

D_MODEL = 2048
BATCH = 2
SEQ = 16384
DEPTH = 2

GRID_W = 64
CTX_LEN = 256

HEAD_DIM = 128
N_Q_HEADS_A = 8
N_KV_HEADS_A = 2
N_Q_HEADS_B = 8
N_KV_HEADS_B = 2
GROUP_A = N_Q_HEADS_A // N_KV_HEADS_A
GROUP_B = N_Q_HEADS_B // N_KV_HEADS_B
A_Q = N_Q_HEADS_A * HEAD_DIM
A_KV = N_KV_HEADS_A * HEAD_DIM
B_Q = N_Q_HEADS_B * HEAD_DIM
B_KV = N_KV_HEADS_B * HEAD_DIM
ATT_IN = A_Q + 2 * A_KV + B_Q + 2 * B_KV
ATT_OUT = A_Q + B_Q
ATT_CUTS = (A_Q, A_Q + A_KV, A_Q + 2 * A_KV, A_Q + 2 * A_KV + B_Q, A_Q + 2 * A_KV + B_Q + B_KV)
Q_BLOCK = 128
WINDOW = 128
BAND = Q_BLOCK + 2 * WINDOW
ROPE_THETA = 10000.0
ATTN_SCALE = HEAD_DIM ** -0.5
NEG_INF = -1e30

REC_EXPAND = 128
REC_HEADS = D_MODEL // REC_EXPAND
REC_DK = REC_EXPAND
REC_DV = D_MODEL // REC_HEADS
REC_KEY = REC_HEADS * REC_DK
REC_VAL = REC_HEADS * REC_DV
REC_IN = 3 * REC_KEY + 2 * REC_VAL
REC_CUTS = (REC_KEY, 2 * REC_KEY, 3 * REC_KEY, 3 * REC_KEY + REC_VAL)
CHUNK = 64

D_FF = 5632
CONV_W = 3

N_MOD = 6
N_ATT_LAYERS = (DEPTH + 1) // 2
N_REC_LAYERS = DEPTH // 2
EPS = 1e-6

kernel_name = "hybrid_dit_gqa_window_hgrn2_convffn"


def rms_norm(x, gain):
    xf = x.astype(jnp.float32)
    y = xf * lax.rsqrt(jnp.mean(xf * xf, axis=-1, keepdims=True) + EPS)
    return (y * gain.astype(jnp.float32)).astype(x.dtype)


def modulate(h, shift, scale):
    return h * (1.0 + scale) + shift


def split_heads(t, n_heads):
    return t.reshape(t.shape[:-1] + (n_heads, t.shape[-1] // n_heads))


def group_q(q, n_kv):
    return q.reshape(q.shape[:2] + (n_kv, q.shape[2] // n_kv, q.shape[3]))


def axial_rope_tables(n_tokens):
    n_rows = n_tokens // GRID_W
    row = jnp.repeat(jnp.arange(n_rows, dtype=jnp.float32), GRID_W)
    col = jnp.tile(jnp.arange(GRID_W, dtype=jnp.float32), n_rows)
    quarter = HEAD_DIM // 4
    inv_freq = ROPE_THETA ** (-jnp.arange(quarter, dtype=jnp.float32) / quarter)
    ang = jnp.concatenate([row[:, None] * inv_freq, col[:, None] * inv_freq], axis=-1)
    return jnp.cos(ang), jnp.sin(ang)


def apply_rope(x, cos, sin):
    half = HEAD_DIM // 2
    x1 = x[..., :half].astype(jnp.float32)
    x2 = x[..., half:].astype(jnp.float32)
    cs = cos[None, :, None, :]
    sn = sin[None, :, None, :]
    return jnp.concatenate([x1 * cs - x2 * sn, x1 * sn + x2 * cs], axis=-1).astype(x.dtype)


def gqa_scores(q, k):
    return jnp.einsum('bqkgd,bskd->bkgqs', q, k, preferred_element_type=jnp.float32) * ATTN_SCALE


def gqa_out(p, v):
    return jnp.einsum('bkgqs,bskd->bqkgd', p.astype(v.dtype), v)


def attention_mixer(h_lat, h_ctx, w_in, w_out, qk_gain, sink, cos, sin, with_ctx_out):
    bsz, n_lat, _ = h_lat.shape
    n_ctx = h_ctx.shape[1]
    n_blk = n_lat // Q_BLOCK

    def project(h, rope):
        qa, ka, va, qb, kb, vb = jnp.split(h @ w_in, ATT_CUTS, axis=-1)
        qa = rms_norm(split_heads(qa, N_Q_HEADS_A), qk_gain[0])
        ka = rms_norm(split_heads(ka, N_KV_HEADS_A), qk_gain[1])
        qb = rms_norm(split_heads(qb, N_Q_HEADS_B), qk_gain[2])
        kb = rms_norm(split_heads(kb, N_KV_HEADS_B), qk_gain[3])
        va = split_heads(va, N_KV_HEADS_A)
        vb = split_heads(vb, N_KV_HEADS_B)
        if rope:
            qa, ka, qb, kb = (apply_rope(t, cos, sin) for t in (qa, ka, qb, kb))
        return group_q(qa, N_KV_HEADS_A), ka, va, group_q(qb, N_KV_HEADS_B), kb, vb

    qa, ka, va, qb, kb, vb = project(h_lat, True)
    qa_c, ka_c, va_c, qb_c, kb_c, vb_c = project(h_ctx, False)
    sink_logit = sink.astype(jnp.float32).reshape(N_KV_HEADS_B, GROUP_B)

    ka_all = jnp.concatenate([ka_c, ka], axis=1)
    va_all = jnp.concatenate([va_c, va], axis=1)
    pad = ((0, 0), (WINDOW, WINDOW), (0, 0), (0, 0))
    kb_pad = jnp.pad(kb, pad)
    vb_pad = jnp.pad(vb, pad)
    qa_blocks = qa.reshape(bsz, n_blk, Q_BLOCK, N_KV_HEADS_A, GROUP_A, HEAD_DIM).swapaxes(0, 1)
    qb_blocks = qb.reshape(bsz, n_blk, Q_BLOCK, N_KV_HEADS_B, GROUP_B, HEAD_DIM).swapaxes(0, 1)
    offs_q = jnp.arange(Q_BLOCK, dtype=jnp.int32)
    offs_k = jnp.arange(BAND, dtype=jnp.int32) - WINDOW

    def block(args):
        qa_i, qb_i, n = args
        start = n * Q_BLOCK
        oa = gqa_out(jax.nn.softmax(gqa_scores(qa_i, ka_all), axis=-1), va_all)
        kb_i = lax.dynamic_slice_in_dim(kb_pad, start, BAND, axis=1)
        vb_i = lax.dynamic_slice_in_dim(vb_pad, start, BAND, axis=1)
        qpos = start + offs_q
        kpos = start + offs_k
        valid = ((jnp.abs(qpos[:, None] - kpos[None, :]) <= WINDOW)
                 & (kpos >= 0)[None, :] & (kpos < n_lat)[None, :])
        s_band = jnp.where(valid, gqa_scores(qb_i, kb_i), NEG_INF)
        s_ctx = gqa_scores(qb_i, kb_c)
        s_sink = jnp.broadcast_to(sink_logit[None, :, :, None, None], s_ctx.shape[:-1] + (1,))
        pb = jax.nn.softmax(jnp.concatenate([s_ctx, s_band, s_sink], axis=-1), axis=-1)
        ob = gqa_out(pb[..., :n_ctx], vb_c) + gqa_out(pb[..., n_ctx:n_ctx + BAND], vb_i)
        return oa, ob

    oa, ob = lax.map(block, (qa_blocks, qb_blocks, jnp.arange(n_blk, dtype=jnp.int32)))
    oa = oa.swapaxes(0, 1).reshape(bsz, n_lat, A_Q)
    ob = ob.swapaxes(0, 1).reshape(bsz, n_lat, B_Q)
    y_lat = jnp.concatenate([oa, ob], axis=-1) @ w_out

    y_ctx = None
    if with_ctx_out:
        oa_c = gqa_out(jax.nn.softmax(gqa_scores(qa_c, ka_c), axis=-1), va_c)
        s = gqa_scores(qb_c, kb_c)
        s_sink = jnp.broadcast_to(sink_logit[None, :, :, None, None], s.shape[:-1] + (1,))
        pb = jax.nn.softmax(jnp.concatenate([s, s_sink], axis=-1), axis=-1)
        ob_c = gqa_out(pb[..., :n_ctx], vb_c)
        y_ctx = jnp.concatenate([oa_c.reshape(bsz, n_ctx, A_Q), ob_c.reshape(bsz, n_ctx, B_Q)], axis=-1) @ w_out
    return y_lat, y_ctx


def forget_gate(f_logit, lb):
    log_f = jnp.logaddexp(jnp.log(lb), jnp.log1p(-lb) + jax.nn.log_sigmoid(f_logit.astype(jnp.float32)))
    return -jnp.expm1(log_f), log_f


def chunk_scan(q, k, v, log_f, s0, with_output):
    bsz, n_tok, n_h, _ = q.shape
    dv = v.shape[-1]
    n_chunk = n_tok // CHUNK

    def chunks(a):
        return a.astype(jnp.float32).reshape(bsz, n_chunk, CHUNK, n_h, a.shape[-1]).transpose(1, 0, 3, 2, 4)

    incl = jnp.tril(jnp.ones((CHUNK, CHUNK), dtype=bool))

    def step(state, xs):
        qc, kc, vc, lc = xs
        cum = jnp.cumsum(lc, axis=2)
        total = cum[:, :, -1:, :]
        new_state = (jnp.exp(total)[:, :, 0, :, None] * state
                     + jnp.einsum('bhsk,bhsv->bhkv', kc * jnp.exp(total - cum), vc))
        if not with_output:
            return new_state, None
        inter = jnp.einsum('bhtk,bhkv->bhtv', qc * jnp.exp(cum), state)
        rel = cum[:, :, :, None, :] - cum[:, :, None, :, :]
        decay = jnp.exp(jnp.where(incl[None, None, :, :, None], rel, -jnp.inf))
        scores = jnp.einsum('bhtk,bhsk,bhtsk->bhts', qc, kc, decay)
        intra = jnp.einsum('bhts,bhsv->bhtv', scores, vc)
        return new_state, inter + intra

    final, outs = lax.scan(step, s0, (chunks(q), chunks(k), chunks(v), chunks(log_f)))
    if not with_output:
        return final, None
    out = outs.transpose(1, 0, 3, 2, 4).reshape(bsz, n_tok, n_h, dv).astype(v.dtype)
    return final, out


def hgrn2_mixer(h_lat, h_ctx, w_in, w_out, out_gain, lb_fwd, lb_bwd, with_ctx_out):
    bsz, n_lat, _ = h_lat.shape
    n_ctx = h_ctx.shape[1]

    def project(h):
        q, f_f, f_b, i, g = jnp.split(h @ w_in, REC_CUTS, axis=-1)
        return (split_heads(q, REC_HEADS), split_heads(f_f, REC_HEADS), split_heads(f_b, REC_HEADS),
                split_heads(i, REC_HEADS), split_heads(g, REC_HEADS))

    q_l, ff_l, fb_l, i_l, g_l = project(h_lat)
    q_c, ff_c, fb_c, i_c, g_c = project(h_ctx)
    lbf = lb_fwd.reshape(REC_HEADS, REC_DK)
    lbb = lb_bwd.reshape(REC_HEADS, REC_DK)
    s0 = jnp.zeros((bsz, REC_HEADS, REC_DK, REC_DV), jnp.float32)
    flip = lambda a: a[:, ::-1]

    k_fc, lf_fc = forget_gate(ff_c, lbf)
    k_fl, lf_fl = forget_gate(ff_l, lbf)
    s_fwd, o_fc = chunk_scan(q_c, k_fc, i_c, lf_fc, s0, with_ctx_out)
    _, o_fl = chunk_scan(q_l, k_fl, i_l, lf_fl, s_fwd, True)
    k_bc, lf_bc = forget_gate(fb_c, lbb)
    k_bl, lf_bl = forget_gate(fb_l, lbb)
    s_bwd, o_bc = chunk_scan(flip(q_c), flip(k_bc), flip(i_c), flip(lf_bc), s0, with_ctx_out)
    _, o_bl = chunk_scan(flip(q_l), flip(k_bl), flip(i_l), flip(lf_bl), s_bwd, True)

    def readout(o, g, n):
        y = rms_norm(o, out_gain) * jax.nn.silu(g)
        return y.reshape(bsz, n, REC_VAL) @ w_out

    y_lat = readout(o_fl + flip(o_bl), g_l, n_lat)
    y_ctx = readout(o_fc + flip(o_bc), g_c, n_ctx) if with_ctx_out else None
    return y_lat, y_ctx


def conv_ffn(h, w_up, conv_w, conv_b, w_down):
    n_tok = h.shape[1]
    u = h @ w_up
    half = CONV_W // 2
    up = jnp.pad(u, ((0, 0), (half, half), (0, 0)))
    u = sum(up[:, j:j + n_tok] * conv_w[j] for j in range(CONV_W)) + conv_b
    gate, val = jnp.split(u, 2, axis=-1)
    return (jax.nn.silu(gate) * val) @ w_down


def setup_inputs(seed: int = 0) -> dict:
    key = jax.random.key(seed)
    ks = jax.random.split(key, 19)

    def nrm(k, shape, std):
        return std * jax.random.normal(k, shape, jnp.float32)

    return {
        "x": nrm(ks[0], (BATCH, SEQ, D_MODEL), 1.0),
        "c": nrm(ks[1], (BATCH, D_MODEL), 1.0),
        "ctx": nrm(ks[2], (BATCH, CTX_LEN, D_MODEL), 1.0),
        "c_ctx": nrm(ks[3], (D_MODEL,), 1.0),
        "w_mod": nrm(ks[4], (DEPTH, D_MODEL, N_MOD * D_MODEL), 0.5 * D_MODEL ** -0.5),
        "b_mod": nrm(ks[5], (DEPTH, N_MOD * D_MODEL), 0.02),
        "norm_gain": 1.0 + nrm(ks[6], (DEPTH, 2, D_MODEL), 0.02),
        "attn_w_in": nrm(ks[7], (N_ATT_LAYERS, D_MODEL, ATT_IN), D_MODEL ** -0.5),
        "attn_w_out": nrm(ks[8], (N_ATT_LAYERS, ATT_OUT, D_MODEL), ATT_OUT ** -0.5),
        "attn_qk_gain": 1.0 + nrm(ks[9], (N_ATT_LAYERS, 4, HEAD_DIM), 0.02),
        "attn_sink": nrm(ks[10], (N_ATT_LAYERS, N_Q_HEADS_B), 0.5),
        "rec_w_in": nrm(ks[11], (N_REC_LAYERS, D_MODEL, REC_IN), D_MODEL ** -0.5),
        "rec_w_out": nrm(ks[12], (N_REC_LAYERS, REC_VAL, D_MODEL), REC_VAL ** -0.5),
        "rec_out_gain": 1.0 + nrm(ks[13], (N_REC_LAYERS, REC_DV), 0.02),
        "rec_lb_gamma": nrm(ks[14], (2, DEPTH, REC_KEY), 0.1),
        "ffn_w_up": nrm(ks[15], (DEPTH, D_MODEL, 2 * D_FF), D_MODEL ** -0.5),
        "ffn_conv_w": nrm(ks[16], (DEPTH, CONV_W, 2 * D_FF), CONV_W ** -0.5),
        "ffn_conv_b": nrm(ks[17], (DEPTH, 2 * D_FF), 0.01),
        "ffn_w_down": nrm(ks[18], (DEPTH, D_FF, D_MODEL), D_FF ** -0.5),
    }


def reference(x, c, ctx, c_ctx, w_mod, b_mod, norm_gain, attn_w_in, attn_w_out, attn_qk_gain, attn_sink,
              rec_w_in, rec_w_out, rec_out_gain, rec_lb_gamma, ffn_w_up, ffn_conv_w, ffn_conv_b, ffn_w_down):
    n_lat = x.shape[1]
    cos, sin = axial_rope_tables(n_lat)
    lb_soft = jax.nn.softmax(rec_lb_gamma.astype(jnp.float32), axis=1)
    lower_bounds = jnp.cumsum(lb_soft, axis=1) - lb_soft[:, :1]
    silu_c = jax.nn.silu(c)
    silu_cc = jax.nn.silu(c_ctx)

    for layer in range(DEPTH):
        last = layer == DEPTH - 1
        mod = jnp.split(silu_c @ w_mod[layer] + b_mod[layer], N_MOD, axis=-1)
        n_cm = 2 if last else N_MOD
        mod_c = jnp.split(silu_cc @ w_mod[layer][:, :n_cm * D_MODEL] + b_mod[layer][:n_cm * D_MODEL], n_cm, axis=-1)

        h = modulate(rms_norm(x, norm_gain[layer, 0]), mod[0][:, None], mod[1][:, None])
        hc = modulate(rms_norm(ctx, norm_gain[layer, 0]), mod_c[0], mod_c[1])
        j = layer // 2
        if layer % 2 == 0:
            y, yc = attention_mixer(h, hc, attn_w_in[j], attn_w_out[j], attn_qk_gain[j], attn_sink[j],
                                    cos, sin, not last)
        else:
            y, yc = hgrn2_mixer(h, hc, rec_w_in[j], rec_w_out[j], rec_out_gain[j],
                                lower_bounds[0, layer], lower_bounds[1, layer], not last)
        x = x + mod[2][:, None] * y
        h = modulate(rms_norm(x, norm_gain[layer, 1]), mod[3][:, None], mod[4][:, None])
        x = x + mod[5][:, None] * conv_ffn(h, ffn_w_up[layer], ffn_conv_w[layer], ffn_conv_b[layer], ffn_w_down[layer])

        if not last:
            ctx = ctx + mod_c[2] * yc
            hc = modulate(rms_norm(ctx, norm_gain[layer, 1]), mod_c[3], mod_c[4])
            ctx = ctx + mod_c[5] * conv_ffn(hc, ffn_w_up[layer], ffn_conv_w[layer], ffn_conv_b[layer], ffn_w_down[layer])
    return x
```

```python
import functools

import numpy as np
import jax
import jax.numpy as jnp
from jax import lax
from jax.experimental import pallas as pl
from jax.experimental.pallas import tpu as pltpu

F32 = jnp.float32
BF16 = jnp.bfloat16

HEAD_DIM = 128
GRID_W = 64
N_Q_HEADS = 8
N_KV_HEADS = 2
GROUP = N_Q_HEADS // N_KV_HEADS
WINDOW = 128
ROPE_THETA = 10000.0
ATTN_SCALE = HEAD_DIM ** -0.5
NEG_INF = -1e30
REC_DK = 128
N_MOD = 6
EPS = 1e-6

LANES = 128
SUBLANES = 8
BF16_ROWS = 16
VMEM_LIMIT = 48 * 1024 * 1024


def _pick(n, pref, align):
    if n <= pref:
        return n
    t = (pref // align) * align
    while t >= align:
        if n % t == 0:
            return t
        t -= align
    raise ValueError(f"no tile for {n} (pref {pref}, align {align})")


def _params(sem):
    return pltpu.CompilerParams(dimension_semantics=sem, vmem_limit_bytes=VMEM_LIMIT)


def _dot(a, b):
    return jnp.dot(a, b, preferred_element_type=F32)


def _dot_nt(a, b):
    return lax.dot_general(a, b, (((1,), (1,)), ((), ())), preferred_element_type=F32)


def _mod_kernel(c_ref, w_ref, b_ref, o_ref):
    cv = c_ref[...]
    o_ref[...] = _dot(cv * jax.nn.sigmoid(cv), w_ref[...]) + b_ref[...]


def _mods(cvec, w_mod, b_mod):
    depth, d, n = w_mod.shape
    tn = _pick(n, 1024, LANES)
    return pl.pallas_call(
        _mod_kernel,
        grid=(depth, n // tn),
        in_specs=[pl.BlockSpec((SUBLANES, d), lambda l, j: (0, 0)),
                  pl.BlockSpec((None, d, tn), lambda l, j: (l, 0, j)),
                  pl.BlockSpec((None, 1, tn), lambda l, j: (l, 0, j))],
        out_specs=pl.BlockSpec((None, SUBLANES, tn), lambda l, j: (l, 0, j)),
        out_shape=jax.ShapeDtypeStruct((depth, SUBLANES, n), F32),
        compiler_params=_params(("arbitrary", "arbitrary")),
        name="adaln_mods",
    )(cvec, w_mod, b_mod.reshape(depth, 1, n))


def _norm_mod(x, g, sh, sc):
    ms = jnp.mean(x * x, axis=-1, keepdims=True)
    return (x * lax.rsqrt(ms + EPS) * g) * (1.0 + sc) + sh


def _fill_h(x_ref, g_ref, sh_ref, sc_ref, h_ref):
    @pl.when(pl.program_id(2) == 0)
    def _():
        h_ref[...] = _norm_mod(x_ref[...], g_ref[...], sh_ref[...], sc_ref[...]).astype(BF16)


def _proj_plain_kernel(x_ref, g_ref, sh_ref, sc_ref, w_ref, o_ref, h_ref):
    _fill_h(x_ref, g_ref, sh_ref, sc_ref, h_ref)
    o_ref[...] = _dot(h_ref[...], w_ref[...]).astype(o_ref.dtype)


def _proj_qk_kernel(*refs, rope):
    if rope:
        x_ref, g_ref, sh_ref, sc_ref, w_ref, hg_ref, cos_ref, sin_ref, o_ref, h_ref = refs
    else:
        x_ref, g_ref, sh_ref, sc_ref, w_ref, hg_ref, o_ref, h_ref = refs
    _fill_h(x_ref, g_ref, sh_ref, sc_ref, h_ref)
    acc = _dot(h_ref[...], w_ref[...])
    for c in range(acc.shape[1] // HEAD_DIM):
        cols = slice(c * HEAD_DIM, (c + 1) * HEAD_DIM)
        xs = acc[:, cols]
        ms = jnp.mean(xs * xs, axis=-1, keepdims=True)
        y = xs * lax.rsqrt(ms + EPS) * hg_ref[:, cols]
        if rope:
            y = y * cos_ref[...] + pltpu.roll(y, HEAD_DIM // 2, 1) * sin_ref[...]
        o_ref[:, cols] = y.astype(o_ref.dtype)


def _proj_gate_kernel(x_ref, g_ref, sh_ref, sc_ref, w_ref, lb_ref, lf_ref, k_ref, h_ref):
    _fill_h(x_ref, g_ref, sh_ref, sc_ref, h_ref)
    z = _dot(h_ref[...], w_ref[...])
    lb = lb_ref[...]
    sig = jax.nn.sigmoid(z)
    lf_ref[...] = jnp.log(lb + (1.0 - lb) * sig)
    k_ref[...] = ((1.0 - lb) * (1.0 - sig)).astype(k_ref.dtype)


def _proj_call(kernel, x, gain, shift, scale, w, extras, extra_specs, out_dtypes, tm_pref=512, tn_pref=512):
    bsz, t, d = x.shape
    n = w.shape[1]
    tm = _pick(t, tm_pref, BF16_ROWS)
    tn = _pick(n, tn_pref, LANES)
    in_specs = [pl.BlockSpec((None, tm, d), lambda b, i, j: (b, i, 0)),
                pl.BlockSpec((1, d), lambda b, i, j: (0, 0)),
                pl.BlockSpec((None, 1, d), lambda b, i, j: (b, 0, 0)),
                pl.BlockSpec((None, 1, d), lambda b, i, j: (b, 0, 0)),
                pl.BlockSpec((d, tn), lambda b, i, j: (0, j))] + [s(tm, tn) for s in extra_specs]
    out_spec = pl.BlockSpec((None, tm, tn), lambda b, i, j: (b, i, j))
    outs = pl.pallas_call(
        kernel,
        grid=(bsz, t // tm, n // tn),
        in_specs=in_specs,
        out_specs=[out_spec] * len(out_dtypes),
        out_shape=[jax.ShapeDtypeStruct((bsz, t, n), dt) for dt in out_dtypes],
        scratch_shapes=[pltpu.VMEM((tm, d), BF16)],
        compiler_params=_params(("arbitrary", "arbitrary", "arbitrary")),
        name=getattr(kernel, "__name__", "proj").strip("_"),
    )(x, gain.reshape(1, d), shift, scale, w, *extras)
    return outs


def _col_spec(tm, tn):
    return pl.BlockSpec((1, tn), lambda b, i, j: (0, j))


def _pos_spec(tm, tn):
    return pl.BlockSpec((tm, HEAD_DIM), lambda b, i, j: (i, 0))


def _proj_plain(x, gain, shift, scale, w, dtype=BF16):
    return _proj_call(_proj_plain_kernel, x, gain, shift, scale, w, (), (), (dtype,))[0]


def _proj_qk(x, gain, shift, scale, w, head_gain, cos=None, sin=None):
    if cos is None:
        kern = functools.partial(_proj_qk_kernel, rope=False)
        kern.__name__ = "proj_qk_ctx"
        return _proj_call(kern, x, gain, shift, scale, w, (head_gain,), (_col_spec,), (BF16,))[0]
    kern = functools.partial(_proj_qk_kernel, rope=True)
    kern.__name__ = "proj_qk_rope"
    return _proj_call(kern, x, gain, shift, scale, w, (head_gain, cos, sin),
                      (_col_spec, _pos_spec, _pos_spec), (BF16,))[0]


def _proj_gate(x, gain, shift, scale, w, lb):
    return _proj_call(_proj_gate_kernel, x, gain, shift, scale, w, (lb,), (_col_spec,), (F32, BF16))


def _mm_res_kernel(a_ref, w_ref, r_ref, gt_ref, o_ref):
    o_ref[...] = r_ref[...] + gt_ref[...] * _dot(a_ref[...], w_ref[...])


def _mm2_res_kernel(a1_ref, a2_ref, w1_ref, w2_ref, r_ref, gt_ref, o_ref):
    acc = _dot(a1_ref[...], w1_ref[...]) + _dot(a2_ref[...], w2_ref[...])
    o_ref[...] = r_ref[...] + gt_ref[...] * acc


def _mm_res(acts, ws, res, gate, tm_pref=512, tn_pref=512):
    bsz, t, n = res.shape
    tm = _pick(t, tm_pref, BF16_ROWS)
    tn = _pick(n, tn_pref, LANES)
    kernel = _mm_res_kernel if len(acts) == 1 else _mm2_res_kernel
    in_specs = ([pl.BlockSpec((None, tm, a.shape[2]), lambda b, i, j: (b, i, 0)) for a in acts]
                + [pl.BlockSpec((w.shape[0], tn), lambda b, i, j: (0, j)) for w in ws]
                + [pl.BlockSpec((None, tm, tn), lambda b, i, j: (b, i, j)),
                   pl.BlockSpec((None, 1, tn), lambda b, i, j: (b, 0, j))])
    return pl.pallas_call(
        kernel,
        grid=(bsz, t // tm, n // tn),
        in_specs=in_specs,
        out_specs=pl.BlockSpec((None, tm, tn), lambda b, i, j: (b, i, j)),
        out_shape=jax.ShapeDtypeStruct((bsz, t, n), F32),
        compiler_params=_params(("arbitrary", "arbitrary", "arbitrary")),
        name="mm_res" if len(acts) == 1 else "mm2_res",
    )(*acts, *ws, res, gate)


def _ffn_up_kernel(xp_ref, x_ref, xn_ref, g_ref, sh_ref, sc_ref, wg_ref, wv_ref, cwg_ref, cwv_ref,
                   cbg_ref, cbv_ref, o_ref, h_ref, *, tm, halo):
    i = pl.program_id(1)
    last = pl.num_programs(1) - 1

    @pl.when(pl.program_id(2) == 0)
    def _():
        g, sh, sc = g_ref[...], sh_ref[...], sc_ref[...]
        hp = jnp.where(i > 0, _norm_mod(xp_ref[...], g, sh, sc), 0.0)
        hn = jnp.where(i < last, _norm_mod(xn_ref[...], g, sh, sc), 0.0)
        h_ref[0:halo, :] = hp.astype(BF16)
        h_ref[halo:halo + tm, :] = _norm_mod(x_ref[...], g, sh, sc).astype(BF16)
        h_ref[halo + tm:, :] = hn.astype(BF16)

    h = h_ref[...]
    rows = tm + 2 * halo

    def branch(w_ref, cw_ref, cb_ref):
        u = _dot(h, w_ref[...])
        u_prev = pltpu.roll(u, 1, 0)[halo:halo + tm]
        u_next = pltpu.roll(u, rows - 1, 0)[halo:halo + tm]
        cw = cw_ref[...]
        return u_prev * cw[0:1] + u[halo:halo + tm] * cw[1:2] + u_next * cw[2:3] + cb_ref[...]

    yg = branch(wg_ref, cwg_ref, cbg_ref)
    yv = branch(wv_ref, cwv_ref, cbv_ref)
    o_ref[...] = (yg * jax.nn.sigmoid(yg) * yv).astype(o_ref.dtype)


def _ffn_up(x, gain, shift, scale, w_up, conv_w, conv_b, tm_pref=512, tn_pref=512):
    bsz, t, d = x.shape
    d_ff = w_up.shape[1] // 2
    halo = BF16_ROWS
    tm = _pick(t, tm_pref, halo)
    tn = _pick(d_ff, tn_pref, LANES)
    nj = d_ff // tn
    r = tm // halo
    n_halo = t // halo
    kern = functools.partial(_ffn_up_kernel, tm=tm, halo=halo)
    conv_b = conv_b.reshape(1, 2 * d_ff)
    vec = lambda b, i, j: (b, 0, 0)
    return pl.pallas_call(
        kern,
        grid=(bsz, t // tm, nj),
        in_specs=[pl.BlockSpec((None, halo, d), lambda b, i, j: (b, jnp.maximum(i * r - 1, 0), 0)),
                  pl.BlockSpec((None, tm, d), lambda b, i, j: (b, i, 0)),
                  pl.BlockSpec((None, halo, d), lambda b, i, j: (b, jnp.minimum((i + 1) * r, n_halo - 1), 0)),
                  pl.BlockSpec((1, d), lambda b, i, j: (0, 0)),
                  pl.BlockSpec((None, 1, d), vec),
                  pl.BlockSpec((None, 1, d), vec),
                  pl.BlockSpec((d, tn), lambda b, i, j: (0, j)),
                  pl.BlockSpec((d, tn), lambda b, i, j: (0, j + nj)),
                  pl.BlockSpec((conv_w.shape[0], tn), lambda b, i, j: (0, j)),
                  pl.BlockSpec((conv_w.shape[0], tn), lambda b, i, j: (0, j + nj)),
                  pl.BlockSpec((1, tn), lambda b, i, j: (0, j)),
                  pl.BlockSpec((1, tn), lambda b, i, j: (0, j + nj))],
        out_specs=pl.BlockSpec((None, tm, tn), lambda b, i, j: (b, i, j)),
        out_shape=jax.ShapeDtypeStruct((bsz, t, d_ff), BF16),
        scratch_shapes=[pltpu.VMEM((tm + 2 * halo, d), BF16)],
        compiler_params=_params(("arbitrary", "arbitrary", "arbitrary")),
        name="ffn_up_conv",
    )(x, x, x, gain.reshape(1, d), shift, scale, w_up, w_up, conv_w, conv_w, conv_b, conv_b)


def _attn_a_kernel(q_ref, k_ref, v_ref, o_ref, qs_ref, m_ref, l_ref, acc_ref, *, tq):
    step = pl.program_id(3)

    @pl.when(step == 0)
    def _():
        for g in range(GROUP):
            qs_ref[g * tq:(g + 1) * tq, :] = q_ref[:, g * HEAD_DIM:(g + 1) * HEAD_DIM]
        m_ref[...] = jnp.full(m_ref.shape, -jnp.inf, F32)
        l_ref[...] = jnp.zeros(l_ref.shape, F32)
        acc_ref[...] = jnp.zeros(acc_ref.shape, F32)

    s = _dot_nt(qs_ref[...], k_ref[...])
    m_prev = m_ref[...]
    m_new = jnp.maximum(m_prev, jnp.max(s, axis=-1, keepdims=True))
    alpha = jnp.exp(m_prev - m_new)
    p = jnp.exp(s - m_new)
    l_ref[...] = alpha * l_ref[...] + jnp.sum(p, axis=-1, keepdims=True)
    acc_ref[...] = alpha * acc_ref[...] + _dot(p.astype(BF16), v_ref[...])
    m_ref[...] = m_new

    @pl.when(step == pl.num_programs(3) - 1)
    def _():
        out = acc_ref[...] / l_ref[...]
        for g in range(GROUP):
            o_ref[:, g * HEAD_DIM:(g + 1) * HEAD_DIM] = out[g * tq:(g + 1) * tq].astype(o_ref.dtype)


def _attn_a(qk, k_all, v_all, tq_pref=256, tk_pref=1280):
    bsz, t, _ = qk.shape
    s = k_all.shape[1]
    tq = _pick(t, tq_pref, BF16_ROWS)
    tk = _pick(s, tk_pref, LANES)
    gw = GROUP * HEAD_DIM
    m = GROUP * tq
    return pl.pallas_call(
        functools.partial(_attn_a_kernel, tq=tq),
        grid=(bsz, N_KV_HEADS, t // tq, s // tk),
        in_specs=[pl.BlockSpec((None, tq, gw), lambda b, h, i, j: (b, i, h)),
                  pl.BlockSpec((None, tk, HEAD_DIM), lambda b, h, i, j: (b, j, h)),
                  pl.BlockSpec((None, tk, HEAD_DIM), lambda b, h, i, j: (b, j, h))],
        out_specs=pl.BlockSpec((None, tq, gw), lambda b, h, i, j: (b, i, h)),
        out_shape=jax.ShapeDtypeStruct((bsz, t, N_KV_HEADS * gw), BF16),
        scratch_shapes=[pltpu.VMEM((m, HEAD_DIM), BF16), pltpu.VMEM((m, 1), F32),
                        pltpu.VMEM((m, 1), F32), pltpu.VMEM((m, HEAD_DIM), F32)],
        compiler_params=_params(("arbitrary", "arbitrary", "arbitrary", "arbitrary")),
        name="attn_global",
    )(qk, k_all, v_all)


def _softmax_out(s, sink, v):
    m = jnp.maximum(jnp.max(s, axis=-1, keepdims=True), sink)
    p = jnp.exp(s - m)
    den = jnp.sum(p, axis=-1, keepdims=True) + jnp.exp(sink - m)
    return _dot(p.astype(BF16), v) / den


def _attn_b_kernel(sink_ref, q_ref, kc_ref, vc_ref, kp_ref, km_ref, kn_ref, vp_ref, vm_ref, vn_ref, o_ref,
                   kcat_ref, vcat_ref, *, tq, n_ctx, n_lat):
    kv = pl.program_id(1)
    start = pl.program_id(2) * tq
    w = WINDOW
    for dst, parts in ((kcat_ref, (kc_ref, kp_ref, km_ref, kn_ref)), (vcat_ref, (vc_ref, vp_ref, vm_ref, vn_ref))):
        off = 0
        for part in parts:
            dst[off:off + part.shape[0], :] = part[...]
            off += part.shape[0]
    nk = n_ctx + tq + 2 * w
    row = lax.broadcasted_iota(jnp.int32, (tq, nk), 0)
    col = lax.broadcasted_iota(jnp.int32, (tq, nk), 1)
    rel = col - n_ctx - w - row
    kpos = start + col - n_ctx - w
    valid = (col < n_ctx) | ((jnp.abs(rel) <= w) & (kpos >= 0) & (kpos < n_lat))
    kcat = kcat_ref[...]
    vcat = vcat_ref[...]
    for g in range(GROUP):
        cols = slice(g * HEAD_DIM, (g + 1) * HEAD_DIM)
        s = jnp.where(valid, _dot_nt(q_ref[:, cols], kcat), NEG_INF)
        o_ref[:, cols] = _softmax_out(s, sink_ref[kv * GROUP + g], vcat).astype(o_ref.dtype)


def _attn_b(qk, v, qk_c, v_c, sink, tq_pref=256):
    bsz, t, _ = qk.shape
    n_ctx = qk_c.shape[1]
    tq = _pick(t, tq_pref, WINDOW)
    r = tq // WINDOW
    n_w = t // WINDOW
    gw = GROUP * HEAD_DIM
    kb, vb = 18, 2
    prev = lambda c: (lambda b, h, i: (b, jnp.maximum(i * r - 1, 0), c + h))
    main = lambda c: (lambda b, h, i: (b, i, c + h))
    nxt = lambda c: (lambda b, h, i: (b, jnp.minimum((i + 1) * r, n_w - 1), c + h))
    nk = n_ctx + tq + 2 * WINDOW
    return pl.pallas_call(
        functools.partial(_attn_b_kernel, tq=tq, n_ctx=n_ctx, n_lat=t),
        grid=(bsz, N_KV_HEADS, t // tq),
        in_specs=[pl.BlockSpec(memory_space=pltpu.SMEM),
                  pl.BlockSpec((None, tq, gw), lambda b, h, i: (b, i, 2 + h)),
                  pl.BlockSpec((None, n_ctx, HEAD_DIM), lambda b, h, i: (b, 0, kb + h)),
                  pl.BlockSpec((None, n_ctx, HEAD_DIM), lambda b, h, i: (b, 0, vb + h)),
                  pl.BlockSpec((None, WINDOW, HEAD_DIM), prev(kb)),
                  pl.BlockSpec((None, tq, HEAD_DIM), main(kb)),
                  pl.BlockSpec((None, WINDOW, HEAD_DIM), nxt(kb)),
                  pl.BlockSpec((None, WINDOW, HEAD_DIM), prev(vb)),
                  pl.BlockSpec((None, tq, HEAD_DIM), main(vb)),
                  pl.BlockSpec((None, WINDOW, HEAD_DIM), nxt(vb))],
        out_specs=pl.BlockSpec((None, tq, gw), lambda b, h, i: (b, i, h)),
        out_shape=jax.ShapeDtypeStruct((bsz, t, N_KV_HEADS * gw), BF16),
        scratch_shapes=[pltpu.VMEM((nk, HEAD_DIM), BF16), pltpu.VMEM((nk, HEAD_DIM), BF16)],
        compiler_params=_params(("arbitrary", "arbitrary", "arbitrary")),
        name="attn_window",
    )(sink, qk, qk_c, v_c, qk, qk, qk, v, v, v)


def _attn_ctx_kernel(sink_ref, q_ref, k_ref, v_ref, o_ref):
    grp = pl.program_id(1)
    k = k_ref[...]
    v = v_ref[...]
    for g in range(GROUP):
        cols = slice(g * HEAD_DIM, (g + 1) * HEAD_DIM)
        s = _dot_nt(q_ref[:, cols], k)
        o_ref[:, cols] = _softmax_out(s, sink_ref[grp * GROUP + g], v).astype(o_ref.dtype)


def _attn_ctx(qk_c, v_c, sink_all):
    bsz, n_ctx, _ = qk_c.shape
    gw = GROUP * HEAD_DIM
    n_grp = 2 * N_KV_HEADS
    return pl.pallas_call(
        _attn_ctx_kernel,
        grid=(bsz, n_grp),
        in_specs=[pl.BlockSpec(memory_space=pltpu.SMEM),
                  pl.BlockSpec((None, n_ctx, gw), lambda b, h: (b, 0, h)),
                  pl.BlockSpec((None, n_ctx, HEAD_DIM), lambda b, h: (b, 0, 16 + h)),
                  pl.BlockSpec((None, n_ctx, HEAD_DIM), lambda b, h: (b, 0, h))],
        out_specs=pl.BlockSpec((None, n_ctx, gw), lambda b, h: (b, 0, h)),
        out_shape=jax.ShapeDtypeStruct((bsz, n_ctx, n_grp * gw), BF16),
        compiler_params=_params(("arbitrary", "arbitrary")),
        name="attn_ctx",
    )(sink_all, qk_c, qk_c, v_c)


def _scan_masks(chunk, reverse):
    t = np.arange(chunk)[:, None]
    s = np.arange(chunk)[None, :]
    masks = []
    h = chunk // 2
    while h >= 1:
        same = (t // (2 * h)) == (s // (2 * h))
        t_hi = (t // h) % 2 == 1
        s_hi = (s // h) % 2 == 1
        masks.append(same & (~t_hi & s_hi if reverse else t_hi & ~s_hi))
        h //= 2
    masks.append(t == s)
    tri = (s >= t) if reverse else (s <= t)
    return np.stack(masks).astype(np.float32), tri.astype(np.float32)


def _ref_rows(cum, h, reverse):
    chunk = cum.shape[0]
    pick = h if reverse else h - 1
    if h >= SUBLANES:
        parts = [jnp.broadcast_to(cum[b * 2 * h + pick:b * 2 * h + pick + 1, :], (2 * h, cum.shape[1]))
                 for b in range(chunk // (2 * h))]
        return parts[0] if len(parts) == 1 else jnp.concatenate(parts, axis=0)
    c3 = cum.reshape(chunk // SUBLANES, SUBLANES, cum.shape[1])
    sub = lax.broadcasted_iota(jnp.int32, c3.shape, 1)
    out = None
    for b in range(SUBLANES // (2 * h)):
        r = b * 2 * h + pick
        rows = jnp.broadcast_to(c3[:, r:r + 1, :], c3.shape)
        out = rows if out is None else jnp.where(sub >= b * 2 * h, rows, out)
    return out.reshape(chunk, cum.shape[1])


def _cumsum_exact(tri, lf):
    p1 = lf.astype(BF16)
    r1 = lf - p1.astype(F32)
    p2 = r1.astype(BF16)
    p3 = (r1 - p2.astype(F32)).astype(BF16)
    return _dot(tri, p1) + _dot(tri, p2) + _dot(tri, p3)


def _chunk_step(st, k, lf, v, tri, reverse, q=None, masks=None):
    chunk = lf.shape[0]
    cum = _cumsum_exact(tri, lf)
    total = cum[0:1, :] if reverse else cum[chunk - 1:chunk, :]
    kf = k.astype(F32)
    k_out = (kf * jnp.exp(total - cum)).astype(BF16)
    v_t = v.astype(F32).T.astype(BF16)
    st_new = st * jnp.exp(total) + _dot(v_t, k_out)
    if q is None:
        return st_new, None
    qf = q.astype(F32)
    out = _dot_nt((qf * jnp.exp(cum)).astype(BF16), st.astype(BF16))
    scores = masks[masks.shape[0] - 1] * _dot_nt(q, k)
    level = 0
    h = chunk // 2
    while h >= 1:
        e = jnp.exp(-jnp.abs(cum - _ref_rows(cum, h, reverse)))
        scores = scores + masks[level] * _dot_nt((qf * e).astype(BF16), (kf * e).astype(BF16))
        level += 1
        h //= 2
    return st_new, out + _dot(scores.astype(BF16), v)


def _chunk_order(n, reverse):
    return range(n - 1, -1, -1) if reverse else range(n)


def _scan_state_kernel(k_ref, lf_ref, v_ref, s0_ref, tri_ref, sfin_ref, st_ref, *, chunk, reverse):
    step = pl.program_id(2)

    @pl.when(step == 0)
    def _():
        st_ref[...] = s0_ref[...]

    tri = tri_ref[...]
    st = st_ref[...]
    for c in _chunk_order(k_ref.shape[0] // chunk, reverse):
        rows = slice(c * chunk, (c + 1) * chunk)
        st, _ = _chunk_step(st, k_ref[rows, :], lf_ref[rows, :], v_ref[rows, :], tri, reverse)
    st_ref[...] = st

    @pl.when(step == pl.num_programs(2) - 1)
    def _():
        sfin_ref[...] = st


def _scan_out_kernel(*refs, chunk, reverse, final):
    if final:
        q_ref, k_ref, lf_ref, v_ref, s0_ref, tri_ref, masks_ref, of_ref, g_ref, gain_ref, o_ref, st_ref = refs
    else:
        q_ref, k_ref, lf_ref, v_ref, s0_ref, tri_ref, masks_ref, o_ref, st_ref = refs

    @pl.when(pl.program_id(2) == 0)
    def _():
        st_ref[...] = s0_ref[...]

    tri = tri_ref[...]
    masks = masks_ref[...]
    st = st_ref[...]
    for c in _chunk_order(k_ref.shape[0] // chunk, reverse):
        rows = slice(c * chunk, (c + 1) * chunk)
        st, out = _chunk_step(st, k_ref[rows, :], lf_ref[rows, :], v_ref[rows, :], tri, reverse,
                              q=q_ref[rows, :], masks=masks)
        if final:
            tot = out + of_ref[rows, :]
            ms = jnp.mean(tot * tot, axis=-1, keepdims=True)
            gate = g_ref[rows, :].astype(F32)
            out = tot * lax.rsqrt(ms + EPS) * gain_ref[...] * (gate * jax.nn.sigmoid(gate))
        o_ref[rows, :] = out.astype(o_ref.dtype)
    st_ref[...] = st


def _scan_geometry(t, blk_pref=512, chunk_pref=256):
    chunk = _pick(t, chunk_pref, LANES)
    blk = _pick(t, blk_pref, chunk)
    return chunk, blk


def _scan_state(kf, lf, qig, s0, direction):
    bsz, t, _ = kf.shape
    n_h = s0.shape[1]
    reverse = direction == 1
    chunk, blk = _scan_geometry(t)
    n_blk = t // blk
    _, tri = _scan_masks(chunk, reverse)
    pos = (lambda c: n_blk - 1 - c) if reverse else (lambda c: c)
    kcol = direction * n_h
    return pl.pallas_call(
        functools.partial(_scan_state_kernel, chunk=chunk, reverse=reverse),
        grid=(bsz, n_h, n_blk),
        in_specs=[pl.BlockSpec((None, blk, REC_DK), lambda b, h, c: (b, pos(c), kcol + h)),
                  pl.BlockSpec((None, blk, REC_DK), lambda b, h, c: (b, pos(c), kcol + h)),
                  pl.BlockSpec((None, blk, REC_DK), lambda b, h, c: (b, pos(c), n_h + h)),
                  pl.BlockSpec((None, None, REC_DK, REC_DK), lambda b, h, c: (b, h, 0, 0)),
                  pl.BlockSpec((chunk, chunk), lambda b, h, c: (0, 0))],
        out_specs=pl.BlockSpec((None, None, REC_DK, REC_DK), lambda b, h, c: (b, h, 0, 0)),
        out_shape=jax.ShapeDtypeStruct(s0.shape, F32),
        scratch_shapes=[pltpu.VMEM((REC_DK, REC_DK), F32)],
        compiler_params=_params(("arbitrary", "arbitrary", "arbitrary")),
        name="hgrn_state_bwd" if reverse else "hgrn_state_fwd",
    )(kf, lf, qig, s0, jnp.asarray(tri, BF16))


def _scan_out(kf, lf, qig, s0, direction, other=None, out_gain=None):
    bsz, t, _ = kf.shape
    n_h = s0.shape[1]
    reverse = direction == 1
    final = other is not None
    chunk, blk = _scan_geometry(t)
    n_blk = t // blk
    masks, tri = _scan_masks(chunk, reverse)
    pos = (lambda c: n_blk - 1 - c) if reverse else (lambda c: c)
    kcol = direction * n_h
    tok = lambda col: pl.BlockSpec((None, blk, REC_DK), lambda b, h, c: (b, pos(c), col + h))
    in_specs = [tok(0), tok(kcol), tok(kcol), tok(n_h),
                pl.BlockSpec((None, None, REC_DK, REC_DK), lambda b, h, c: (b, h, 0, 0)),
                pl.BlockSpec((chunk, chunk), lambda b, h, c: (0, 0)),
                pl.BlockSpec(masks.shape, lambda b, h, c: (0, 0, 0))]
    args = [qig, kf, lf, qig, s0, jnp.asarray(tri, BF16), jnp.asarray(masks)]
    if final:
        in_specs += [tok(0), tok(2 * n_h), pl.BlockSpec((1, REC_DK), lambda b, h, c: (0, 0))]
        args += [other, qig, out_gain.reshape(1, REC_DK)]
    return pl.pallas_call(
        functools.partial(_scan_out_kernel, chunk=chunk, reverse=reverse, final=final),
        grid=(bsz, n_h, n_blk),
        in_specs=in_specs,
        out_specs=tok(0),
        out_shape=jax.ShapeDtypeStruct((bsz, t, n_h * REC_DK), BF16 if final else F32),
        scratch_shapes=[pltpu.VMEM((REC_DK, REC_DK), F32)],
        compiler_params=_params(("arbitrary", "arbitrary", "arbitrary")),
        name="hgrn_scan_bwd" if reverse else "hgrn_scan_fwd",
    )(*args)


def _rope_tables(n_tokens):
    n_rows = n_tokens // GRID_W
    row = jnp.repeat(jnp.arange(n_rows, dtype=F32), GRID_W)
    col = jnp.tile(jnp.arange(GRID_W, dtype=F32), n_rows)
    quarter = HEAD_DIM // 4
    inv_freq = ROPE_THETA ** (-jnp.arange(quarter, dtype=F32) / quarter)
    ang = jnp.concatenate([row[:, None] * inv_freq, col[:, None] * inv_freq], axis=-1)
    cos, sin = jnp.cos(ang), jnp.sin(ang)
    return jnp.concatenate([cos, cos], axis=-1), jnp.concatenate([-sin, sin], axis=-1)


def _conv_ffn(x, gain, shift, scale, gate, w_up, conv_w, conv_b, w_down):
    act = _ffn_up(x, gain, shift, scale, w_up, conv_w, conv_b)
    return _mm_res([act], [w_down], x, gate)


def kernel(x, c, ctx, c_ctx, w_mod, b_mod, norm_gain, attn_w_in, attn_w_out, attn_qk_gain, attn_sink,
           rec_w_in, rec_w_out, rec_out_gain, rec_lb_gamma, ffn_w_up, ffn_conv_w, ffn_conv_b, ffn_w_down):
    bsz, n_lat, d = x.shape
    depth = w_mod.shape[0]
    assert depth == 2 and bsz < SUBLANES, "layer 0 attention, layer 1 recurrence; conditioning rows fit one tile"
    n_heads_rec = d // REC_DK

    cvec = jnp.zeros((SUBLANES, d), F32).at[:bsz].set(c).at[bsz].set(c_ctx)
    mods = _mods(cvec, w_mod, b_mod)

    def lat_mod(layer, k):
        return mods[layer, :bsz, k * d:(k + 1) * d][:, None, :]

    def ctx_mod(layer, k):
        return jnp.broadcast_to(mods[layer, bsz, k * d:(k + 1) * d][None, None, :], (bsz, 1, d))

    w_up = ffn_w_up.astype(BF16)
    w_down = ffn_w_down.astype(BF16)

    aq = N_Q_HEADS * HEAD_DIM
    akv = N_KV_HEADS * HEAD_DIM
    w_in = attn_w_in[0]
    qa, ka, va, qb, kb, vb = jnp.split(w_in, np.cumsum([aq, akv, akv, aq, akv])[:5].tolist(), axis=1)
    w_qk = jnp.concatenate([qa, qb, ka, kb], axis=1).astype(BF16)
    w_v = jnp.concatenate([va, vb], axis=1).astype(BF16)
    qkg = attn_qk_gain[0]
    head_gain = jnp.concatenate([jnp.tile(qkg[0] * ATTN_SCALE, N_Q_HEADS), jnp.tile(qkg[2] * ATTN_SCALE, N_Q_HEADS),
                                 jnp.tile(qkg[1], N_KV_HEADS), jnp.tile(qkg[3], N_KV_HEADS)])[None, :]
    cos, sin = _rope_tables(n_lat)
    g0 = norm_gain[0, 0]
    qk_l = _proj_qk(x, g0, lat_mod(0, 0), lat_mod(0, 1), w_qk, head_gain, cos, sin)
    v_l = _proj_plain(x, g0, lat_mod(0, 0), lat_mod(0, 1), w_v)
    qk_c = _proj_qk(ctx, g0, ctx_mod(0, 0), ctx_mod(0, 1), w_qk, head_gain)
    v_c = _proj_plain(ctx, g0, ctx_mod(0, 0), ctx_mod(0, 1), w_v)

    ka_cols = slice(2 * aq, 2 * aq + akv)
    k_all = jnp.concatenate([qk_c[:, :, ka_cols], qk_l[:, :, ka_cols]], axis=1)
    v_all = jnp.concatenate([v_c[:, :, :akv], v_l[:, :, :akv]], axis=1)
    sink = attn_sink[0].astype(F32)
    o_a = _attn_a(qk_l, k_all, v_all)
    o_b = _attn_b(qk_l, v_l, qk_c, v_c, sink)
    o_c = _attn_ctx(qk_c, v_c, jnp.concatenate([jnp.full((N_Q_HEADS,), -jnp.inf, F32), sink]))

    w_out = attn_w_out[0].astype(BF16)
    x = _mm_res([o_a, o_b], [w_out[:aq], w_out[aq:]], x, lat_mod(0, 2))
    ctx = _mm_res([o_c], [w_out], ctx, ctx_mod(0, 2))
    g1 = norm_gain[0, 1]
    x = _conv_ffn(x, g1, lat_mod(0, 3), lat_mod(0, 4), lat_mod(0, 5), w_up[0], ffn_conv_w[0], ffn_conv_b[0], w_down[0])
    ctx = _conv_ffn(ctx, g1, ctx_mod(0, 3), ctx_mod(0, 4), ctx_mod(0, 5), w_up[0], ffn_conv_w[0], ffn_conv_b[0],
                    w_down[0])

    lb_soft = jax.nn.softmax(rec_lb_gamma.astype(F32), axis=1)
    lower = (jnp.cumsum(lb_soft, axis=1) - lb_soft[:, :1])[:, 1]
    rk = n_heads_rec * REC_DK
    w_q, w_ff, w_fb, w_i, w_g = jnp.split(rec_w_in[0], [rk, 2 * rk, 3 * rk, 3 * rk + d], axis=1)
    w_qig = jnp.concatenate([w_q, w_i, w_g], axis=1).astype(BF16)
    w_f = jnp.concatenate([w_ff, w_fb], axis=1).astype(BF16)
    lb = lower.reshape(1, 2 * rk)
    g0 = norm_gain[1, 0]
    qig_l = _proj_plain(x, g0, lat_mod(1, 0), lat_mod(1, 1), w_qig)
    lf_l, kf_l = _proj_gate(x, g0, lat_mod(1, 0), lat_mod(1, 1), w_f, lb)
    qig_c = _proj_plain(ctx, g0, ctx_mod(1, 0), ctx_mod(1, 1), w_qig)
    lf_c, kf_c = _proj_gate(ctx, g0, ctx_mod(1, 0), ctx_mod(1, 1), w_f, lb)

    s_zero = jnp.zeros((bsz, n_heads_rec, REC_DK, REC_DK), F32)
    s_fwd = _scan_state(kf_c, lf_c, qig_c, s_zero, 0)
    s_bwd = _scan_state(kf_c, lf_c, qig_c, s_zero, 1)
    o_f = _scan_out(kf_l, lf_l, qig_l, s_fwd, 0)
    y = _scan_out(kf_l, lf_l, qig_l, s_bwd, 1, other=o_f, out_gain=rec_out_gain[0])
    x = _mm_res([y], [rec_w_out[0].astype(BF16)], x, lat_mod(1, 2))
    g1 = norm_gain[1, 1]
    x = _conv_ffn(x, g1, lat_mod(1, 3), lat_mod(1, 4), lat_mod(1, 5), w_up[1], ffn_conv_w[1], ffn_conv_b[1], w_down[1])
    return x
```

```python
import functools

import numpy as np
import jax
import jax.numpy as jnp
from jax import lax
from jax.experimental import pallas as pl
from jax.experimental.pallas import tpu as pltpu

F32 = jnp.float32
BF16 = jnp.bfloat16

HEAD_DIM = 128
GRID_W = 64
N_Q_HEADS = 8
N_KV_HEADS = 2
GROUP = N_Q_HEADS // N_KV_HEADS
WINDOW = 128
ROPE_THETA = 10000.0
ATTN_SCALE = HEAD_DIM ** -0.5
NEG_INF = -1e30
LOG2E = 1.4426950408889634
REC_DK = 128
N_MOD = 6
EPS = 1e-6

LANES = 128
SUBLANES = 8
BF16_ROWS = 16
VMEM_LIMIT = 56 * 1024 * 1024
W_TILE_BYTES = 6 * 1024 * 1024


def _pick(n, pref, align):
    if n <= pref:
        return n
    t = (pref // align) * align
    while t >= align:
        if n % t == 0:
            return t
        t -= align
    raise ValueError(f"no tile for {n} (pref {pref}, align {align})")


def _params(sem):
    return pltpu.CompilerParams(dimension_semantics=sem, vmem_limit_bytes=VMEM_LIMIT)


def _dot(a, b):
    return jnp.dot(a, b, preferred_element_type=F32)


def _dot_nt(a, b):
    return lax.dot_general(a, b, (((1,), (1,)), ((), ())), preferred_element_type=F32)


def _mod_kernel(c_ref, w_ref, b_ref, o_ref):
    cv = c_ref[...]
    o_ref[...] = _dot(cv * jax.nn.sigmoid(cv), w_ref[...]) + b_ref[...]


def _mods(cvec, w_mod, b_mod):
    depth, d, n = w_mod.shape
    tn = _pick(n, 1024, LANES)
    return pl.pallas_call(
        _mod_kernel,
        grid=(depth, n // tn),
        in_specs=[pl.BlockSpec((SUBLANES, d), lambda l, j: (0, 0)),
                  pl.BlockSpec((None, d, tn), lambda l, j: (l, 0, j)),
                  pl.BlockSpec((None, 1, tn), lambda l, j: (l, 0, j))],
        out_specs=pl.BlockSpec((None, SUBLANES, tn), lambda l, j: (l, 0, j)),
        out_shape=jax.ShapeDtypeStruct((depth, SUBLANES, n), F32),
        compiler_params=_params(("arbitrary", "arbitrary")),
        name="adaln_mods",
    )(cvec, w_mod, b_mod.reshape(depth, 1, n))


def _norm_mod(x, g, sh, sc):
    ms = jnp.mean(x * x, axis=-1, keepdims=True)
    return (x * lax.rsqrt(ms + EPS)) * (g * (1.0 + sc)) + sh


def _fill_h(x_ref, g_ref, sh_ref, sc_ref, h_ref):
    @pl.when(pl.program_id(2) == 0)
    def _():
        h_ref[...] = _norm_mod(x_ref[...], g_ref[...], sh_ref[...], sc_ref[...]).astype(BF16)


def _proj_plain_kernel(x_ref, g_ref, sh_ref, sc_ref, w_ref, o_ref, h_ref):
    _fill_h(x_ref, g_ref, sh_ref, sc_ref, h_ref)
    o_ref[...] = _dot(h_ref[...], w_ref[...]).astype(o_ref.dtype)


def _proj_qk_kernel(*refs, rope):
    if rope:
        x_ref, g_ref, sh_ref, sc_ref, w_ref, hg_ref, cos_ref, sin_ref, o_ref, h_ref = refs
    else:
        x_ref, g_ref, sh_ref, sc_ref, w_ref, hg_ref, o_ref, h_ref = refs
    _fill_h(x_ref, g_ref, sh_ref, sc_ref, h_ref)
    acc = _dot(h_ref[...], w_ref[...])
    for c in range(acc.shape[1] // HEAD_DIM):
        cols = slice(c * HEAD_DIM, (c + 1) * HEAD_DIM)
        xs = acc[:, cols]
        ms = jnp.mean(xs * xs, axis=-1, keepdims=True)
        y = xs * lax.rsqrt(ms + EPS) * hg_ref[:, cols]
        if rope:
            y = y * cos_ref[...] + pltpu.roll(y, HEAD_DIM // 2, 1) * sin_ref[...]
        o_ref[:, cols] = y.astype(o_ref.dtype)


def _proj_gate_kernel(x_ref, g_ref, sh_ref, sc_ref, w_ref, lb_ref, lf_ref, k_ref, h_ref):
    _fill_h(x_ref, g_ref, sh_ref, sc_ref, h_ref)
    z = _dot(h_ref[...], w_ref[...])
    lb = lb_ref[...]
    sig = jax.nn.sigmoid(z)
    lf_ref[...] = jnp.log(lb + (1.0 - lb) * sig)
    k_ref[...] = ((1.0 - lb) * (1.0 - sig)).astype(k_ref.dtype)


def _proj_call(kernel, x, gain, shift, scale, w, extras, extra_specs, out_dtypes, tm_pref=1024, tn_pref=1024):
    bsz, t, d = x.shape
    n = w.shape[1]
    tm = _pick(t, tm_pref, BF16_ROWS)
    tn = _pick(n, tn_pref, LANES)
    in_specs = [pl.BlockSpec((None, tm, d), lambda b, i, j: (b, i, 0)),
                pl.BlockSpec((1, d), lambda b, i, j: (0, 0)),
                pl.BlockSpec((None, 1, d), lambda b, i, j: (b, 0, 0)),
                pl.BlockSpec((None, 1, d), lambda b, i, j: (b, 0, 0)),
                pl.BlockSpec((d, tn), lambda b, i, j: (0, j))] + [s(tm, tn) for s in extra_specs]
    out_spec = pl.BlockSpec((None, tm, tn), lambda b, i, j: (b, i, j))
    outs = pl.pallas_call(
        kernel,
        grid=(bsz, t // tm, n // tn),
        in_specs=in_specs,
        out_specs=[out_spec] * len(out_dtypes),
        out_shape=[jax.ShapeDtypeStruct((bsz, t, n), dt) for dt in out_dtypes],
        scratch_shapes=[pltpu.VMEM((tm, d), BF16)],
        compiler_params=_params(("arbitrary", "arbitrary", "arbitrary")),
        name=getattr(kernel, "__name__", "proj").strip("_"),
    )(x, gain.reshape(1, d), shift, scale, w, *extras)
    return outs


def _col_spec(tm, tn):
    return pl.BlockSpec((1, tn), lambda b, i, j: (0, j))


def _pos_spec(tm, tn):
    return pl.BlockSpec((tm, HEAD_DIM), lambda b, i, j: (i, 0))


def _proj_plain(x, gain, shift, scale, w, dtype=BF16):
    return _proj_call(_proj_plain_kernel, x, gain, shift, scale, w, (), (), (dtype,))[0]


def _proj_qk(x, gain, shift, scale, w, head_gain, cos=None, sin=None):
    if cos is None:
        kern = functools.partial(_proj_qk_kernel, rope=False)
        kern.__name__ = "proj_qk_ctx"
        return _proj_call(kern, x, gain, shift, scale, w, (head_gain,), (_col_spec,), (BF16,))[0]
    kern = functools.partial(_proj_qk_kernel, rope=True)
    kern.__name__ = "proj_qk_rope"
    return _proj_call(kern, x, gain, shift, scale, w, (head_gain, cos, sin),
                      (_col_spec, _pos_spec, _pos_spec), (BF16,))[0]


def _proj_gate(x, gain, shift, scale, w, lb):
    return _proj_call(_proj_gate_kernel, x, gain, shift, scale, w, (lb,), (_col_spec,), (F32, BF16))


def _mm_res_kernel(a_ref, w_ref, r_ref, gt_ref, o_ref):
    o_ref[...] = r_ref[...] + gt_ref[...] * _dot(a_ref[...], w_ref[...])


def _mm2_res_kernel(a1_ref, a2_ref, w1_ref, w2_ref, r_ref, gt_ref, o_ref):
    acc = _dot(a1_ref[...], w1_ref[...]) + _dot(a2_ref[...], w2_ref[...])
    o_ref[...] = r_ref[...] + gt_ref[...] * acc


def _mm_res(acts, ws, res, gate, tm_pref=1024):
    bsz, t, n = res.shape
    tm = _pick(t, tm_pref, BF16_ROWS)
    k_total = sum(w.shape[0] for w in ws)
    tn = _pick(n, max(LANES, W_TILE_BYTES // (2 * k_total)), LANES)
    kernel = _mm_res_kernel if len(acts) == 1 else _mm2_res_kernel
    in_specs = ([pl.BlockSpec((None, tm, a.shape[2]), lambda b, i, j: (b, i, 0)) for a in acts]
                + [pl.BlockSpec((w.shape[0], tn), lambda b, i, j: (0, j)) for w in ws]
                + [pl.BlockSpec((None, tm, tn), lambda b, i, j: (b, i, j)),
                   pl.BlockSpec((None, 1, tn), lambda b, i, j: (b, 0, j))])
    return pl.pallas_call(
        kernel,
        grid=(bsz, t // tm, n // tn),
        in_specs=in_specs,
        out_specs=pl.BlockSpec((None, tm, tn), lambda b, i, j: (b, i, j)),
        out_shape=jax.ShapeDtypeStruct((bsz, t, n), F32),
        compiler_params=_params(("arbitrary", "arbitrary", "arbitrary")),
        name="mm_res" if len(acts) == 1 else "mm2_res",
    )(*acts, *ws, res, gate)


def _ffn_up_kernel(xp_ref, x_ref, xn_ref, g_ref, sh_ref, sc_ref, wg_ref, wv_ref, cwg_ref, cwv_ref,
                   cbg_ref, cbv_ref, o_ref, h_ref, *, tm, halo):
    i = pl.program_id(1)
    last = pl.num_programs(1) - 1

    @pl.when(pl.program_id(2) == 0)
    def _():
        g, sh, sc = g_ref[...], sh_ref[...], sc_ref[...]
        hp = jnp.where(i > 0, _norm_mod(xp_ref[...], g, sh, sc), 0.0)
        hn = jnp.where(i < last, _norm_mod(xn_ref[...], g, sh, sc), 0.0)
        h_ref[0:halo, :] = hp.astype(BF16)
        h_ref[halo:halo + tm, :] = _norm_mod(x_ref[...], g, sh, sc).astype(BF16)
        h_ref[halo + tm:, :] = hn.astype(BF16)

    h = h_ref[...]
    rows = tm + 2 * halo

    def branch(w_ref, cw_ref, cb_ref):
        u = _dot(h, w_ref[...])
        u_prev = pltpu.roll(u, 1, 0)[halo:halo + tm]
        u_next = pltpu.roll(u, rows - 1, 0)[halo:halo + tm]
        cw = cw_ref[...]
        return u_prev * cw[0:1] + u[halo:halo + tm] * cw[1:2] + u_next * cw[2:3] + cb_ref[...]

    yg = branch(wg_ref, cwg_ref, cbg_ref)
    yv = branch(wv_ref, cwv_ref, cbv_ref)
    o_ref[...] = (yg * jax.nn.sigmoid(yg) * yv).astype(o_ref.dtype)


def _ffn_up(x, gain, shift, scale, w_up, conv_w, conv_b, tm_pref=1024, tn_pref=512):
    bsz, t, d = x.shape
    d_ff = w_up.shape[1] // 2
    halo = BF16_ROWS
    tm = _pick(t, tm_pref, halo)
    tn = _pick(d_ff, tn_pref, LANES)
    nj = d_ff // tn
    r = tm // halo
    n_halo = t // halo
    kern = functools.partial(_ffn_up_kernel, tm=tm, halo=halo)
    conv_b = conv_b.reshape(1, 2 * d_ff)
    vec = lambda b, i, j: (b, 0, 0)
    return pl.pallas_call(
        kern,
        grid=(bsz, t // tm, nj),
        in_specs=[pl.BlockSpec((None, halo, d), lambda b, i, j: (b, jnp.maximum(i * r - 1, 0), 0)),
                  pl.BlockSpec((None, tm, d), lambda b, i, j: (b, i, 0)),
                  pl.BlockSpec((None, halo, d), lambda b, i, j: (b, jnp.minimum((i + 1) * r, n_halo - 1), 0)),
                  pl.BlockSpec((1, d), lambda b, i, j: (0, 0)),
                  pl.BlockSpec((None, 1, d), vec),
                  pl.BlockSpec((None, 1, d), vec),
                  pl.BlockSpec((d, tn), lambda b, i, j: (0, j)),
                  pl.BlockSpec((d, tn), lambda b, i, j: (0, j + nj)),
                  pl.BlockSpec((conv_w.shape[0], tn), lambda b, i, j: (0, j)),
                  pl.BlockSpec((conv_w.shape[0], tn), lambda b, i, j: (0, j + nj)),
                  pl.BlockSpec((1, tn), lambda b, i, j: (0, j)),
                  pl.BlockSpec((1, tn), lambda b, i, j: (0, j + nj))],
        out_specs=pl.BlockSpec((None, tm, tn), lambda b, i, j: (b, i, j)),
        out_shape=jax.ShapeDtypeStruct((bsz, t, d_ff), BF16),
        scratch_shapes=[pltpu.VMEM((tm + 2 * halo, d), BF16)],
        compiler_params=_params(("arbitrary", "arbitrary", "arbitrary")),
        name="ffn_up_conv",
    )(x, x, x, gain.reshape(1, d), shift, scale, w_up, w_up, conv_w, conv_w, conv_b, conv_b)


def _attn_a_kernel(q_ref, k_ref, v_ref, o_ref, m_ref, acc_ref, *, tq, rs):
    step = pl.program_id(3)

    @pl.when(step == 0)
    def _():
        m_ref[...] = jnp.full(m_ref.shape, -jnp.inf, F32)
        acc_ref[...] = jnp.zeros(acc_ref.shape, F32)

    k = k_ref[...]
    v = v_ref[...]
    n_chunk = k.shape[0] // LANES
    for g in range(GROUP):
        for r in range(tq // rs):
            rows = slice(g * tq + r * rs, g * tq + (r + 1) * rs)
            q = q_ref[r * rs:(r + 1) * rs, g * HEAD_DIM:(g + 1) * HEAD_DIM]
            s = _dot_nt(q, k)
            chunks = [s[:, c * LANES:(c + 1) * LANES] for c in range(n_chunk)]
            smax = functools.reduce(jnp.maximum, chunks)
            m_prev = m_ref[rows, :]
            m_new = jnp.maximum(m_prev, jnp.max(smax, axis=-1, keepdims=True))
            alpha = jnp.exp2(m_prev - m_new)
            p = jnp.concatenate([jnp.exp2(ch - m_new) for ch in chunks], axis=1).astype(BF16)
            acc_ref[rows, :] = jnp.concatenate([alpha, alpha], axis=1) * acc_ref[rows, :] + _dot(p, v)
            m_ref[rows, :] = m_new

    @pl.when(step == pl.num_programs(3) - 1)
    def _():
        for g in range(GROUP):
            acc = acc_ref[g * tq:(g + 1) * tq, :]
            o_ref[:, g * HEAD_DIM:(g + 1) * HEAD_DIM] = (acc[:, :HEAD_DIM] / acc[:, HEAD_DIM:]).astype(o_ref.dtype)


def _attn_a(qk, k_all, v_ext, tq_pref=1024, tk_pref=3328, rs_pref=256):
    bsz, t, _ = qk.shape
    s = k_all.shape[1]
    tq = _pick(t, tq_pref, BF16_ROWS)
    rs = _pick(tq, rs_pref, BF16_ROWS)
    tk = _pick(s, tk_pref, LANES)
    gw = GROUP * HEAD_DIM
    m = GROUP * tq
    return pl.pallas_call(
        functools.partial(_attn_a_kernel, tq=tq, rs=rs),
        grid=(bsz, N_KV_HEADS, t // tq, s // tk),
        in_specs=[pl.BlockSpec((None, tq, gw), lambda b, h, i, j: (b, i, h)),
                  pl.BlockSpec((None, tk, HEAD_DIM), lambda b, h, i, j: (b, j, h)),
                  pl.BlockSpec((None, tk, 2 * HEAD_DIM), lambda b, h, i, j: (b, j, h))],
        out_specs=pl.BlockSpec((None, tq, gw), lambda b, h, i, j: (b, i, h)),
        out_shape=jax.ShapeDtypeStruct((bsz, t, N_KV_HEADS * gw), BF16),
        scratch_shapes=[pltpu.VMEM((m, HEAD_DIM), F32), pltpu.VMEM((m, 2 * HEAD_DIM), F32)],
        compiler_params=_params(("arbitrary", "arbitrary", "arbitrary", "arbitrary")),
        name="attn_global",
    )(qk, k_all, v_ext)


def _softmax_out(s, sink, v):
    sink = sink * LOG2E
    m = jnp.maximum(jnp.max(s, axis=-1, keepdims=True), sink)
    p = jnp.exp2(s - m)
    den = jnp.sum(p, axis=-1, keepdims=True) + jnp.exp2(sink - m)
    return _dot(p.astype(BF16), v) / den


def _attn_b_kernel(sink_ref, q_ref, kc_ref, vc_ref, kp_ref, km_ref, kn_ref, vp_ref, vm_ref, vn_ref, o_ref,
                   kcat_ref, vcat_ref, *, tq, n_ctx, n_lat):
    kv = pl.program_id(1)
    start = pl.program_id(2) * tq
    w = WINDOW
    for dst, parts in ((kcat_ref, (kc_ref, kp_ref, km_ref, kn_ref)), (vcat_ref, (vc_ref, vp_ref, vm_ref, vn_ref))):
        off = 0
        for part in parts:
            dst[off:off + part.shape[0], :] = part[...]
            off += part.shape[0]
    nk = n_ctx + tq + 2 * w
    row = lax.broadcasted_iota(jnp.int32, (tq, nk), 0)
    col = lax.broadcasted_iota(jnp.int32, (tq, nk), 1)
    rel = col - n_ctx - w - row
    kpos = start + col - n_ctx - w
    valid = (col < n_ctx) | ((jnp.abs(rel) <= w) & (kpos >= 0) & (kpos < n_lat))
    kcat = kcat_ref[...]
    vcat = vcat_ref[...]
    for g in range(GROUP):
        cols = slice(g * HEAD_DIM, (g + 1) * HEAD_DIM)
        s = jnp.where(valid, _dot_nt(q_ref[:, cols], kcat), NEG_INF)
        o_ref[:, cols] = _softmax_out(s, sink_ref[kv * GROUP + g], vcat).astype(o_ref.dtype)


def _attn_b(qk, v, qk_c, v_c, sink, tq_pref=256):
    bsz, t, _ = qk.shape
    n_ctx = qk_c.shape[1]
    tq = _pick(t, tq_pref, WINDOW)
    r = tq // WINDOW
    n_w = t // WINDOW
    gw = GROUP * HEAD_DIM
    kb, vb = 18, 2
    prev = lambda c: (lambda b, h, i: (b, jnp.maximum(i * r - 1, 0), c + h))
    main = lambda c: (lambda b, h, i: (b, i, c + h))
    nxt = lambda c: (lambda b, h, i: (b, jnp.minimum((i + 1) * r, n_w - 1), c + h))
    nk = n_ctx + tq + 2 * WINDOW
    return pl.pallas_call(
        functools.partial(_attn_b_kernel, tq=tq, n_ctx=n_ctx, n_lat=t),
        grid=(bsz, N_KV_HEADS, t // tq),
        in_specs=[pl.BlockSpec(memory_space=pltpu.SMEM),
                  pl.BlockSpec((None, tq, gw), lambda b, h, i: (b, i, 2 + h)),
                  pl.BlockSpec((None, n_ctx, HEAD_DIM), lambda b, h, i: (b, 0, kb + h)),
                  pl.BlockSpec((None, n_ctx, HEAD_DIM), lambda b, h, i: (b, 0, vb + h)),
                  pl.BlockSpec((None, WINDOW, HEAD_DIM), prev(kb)),
                  pl.BlockSpec((None, tq, HEAD_DIM), main(kb)),
                  pl.BlockSpec((None, WINDOW, HEAD_DIM), nxt(kb)),
                  pl.BlockSpec((None, WINDOW, HEAD_DIM), prev(vb)),
                  pl.BlockSpec((None, tq, HEAD_DIM), main(vb)),
                  pl.BlockSpec((None, WINDOW, HEAD_DIM), nxt(vb))],
        out_specs=pl.BlockSpec((None, tq, gw), lambda b, h, i: (b, i, h)),
        out_shape=jax.ShapeDtypeStruct((bsz, t, N_KV_HEADS * gw), BF16),
        scratch_shapes=[pltpu.VMEM((nk, HEAD_DIM), BF16), pltpu.VMEM((nk, HEAD_DIM), BF16)],
        compiler_params=_params(("arbitrary", "arbitrary", "arbitrary")),
        name="attn_window",
    )(sink, qk, qk_c, v_c, qk, qk, qk, v, v, v)


def _attn_ctx_kernel(sink_ref, q_ref, k_ref, v_ref, o_ref):
    grp = pl.program_id(1)
    k = k_ref[...]
    v = v_ref[...]
    for g in range(GROUP):
        cols = slice(g * HEAD_DIM, (g + 1) * HEAD_DIM)
        s = _dot_nt(q_ref[:, cols], k)
        o_ref[:, cols] = _softmax_out(s, sink_ref[grp * GROUP + g], v).astype(o_ref.dtype)


def _attn_ctx(qk_c, v_c, sink_all):
    bsz, n_ctx, _ = qk_c.shape
    gw = GROUP * HEAD_DIM
    n_grp = 2 * N_KV_HEADS
    return pl.pallas_call(
        _attn_ctx_kernel,
        grid=(bsz, n_grp),
        in_specs=[pl.BlockSpec(memory_space=pltpu.SMEM),
                  pl.BlockSpec((None, n_ctx, gw), lambda b, h: (b, 0, h)),
                  pl.BlockSpec((None, n_ctx, HEAD_DIM), lambda b, h: (b, 0, 16 + h)),
                  pl.BlockSpec((None, n_ctx, HEAD_DIM), lambda b, h: (b, 0, h))],
        out_specs=pl.BlockSpec((None, n_ctx, gw), lambda b, h: (b, 0, h)),
        out_shape=jax.ShapeDtypeStruct((bsz, n_ctx, n_grp * gw), BF16),
        compiler_params=_params(("arbitrary", "arbitrary")),
        name="attn_ctx",
    )(sink_all, qk_c, qk_c, v_c)


def _scan_masks(chunk, reverse):
    t = np.arange(chunk)[:, None]
    s = np.arange(chunk)[None, :]
    masks = []
    h = chunk // 2
    while h >= 1:
        same = (t // (2 * h)) == (s // (2 * h))
        t_hi = (t // h) % 2 == 1
        s_hi = (s // h) % 2 == 1
        masks.append(same & (~t_hi & s_hi if reverse else t_hi & ~s_hi))
        h //= 2
    masks.append(t == s)
    tri = (s >= t) if reverse else (s <= t)
    return np.stack(masks).astype(np.float32), tri.astype(np.float32)


def _ref_rows(cum, h, reverse):
    chunk = cum.shape[0]
    pick = h if reverse else h - 1
    if h >= SUBLANES:
        parts = [jnp.broadcast_to(cum[b * 2 * h + pick:b * 2 * h + pick + 1, :], (2 * h, cum.shape[1]))
                 for b in range(chunk // (2 * h))]
        return parts[0] if len(parts) == 1 else jnp.concatenate(parts, axis=0)
    c3 = cum.reshape(chunk // SUBLANES, SUBLANES, cum.shape[1])
    sub = lax.broadcasted_iota(jnp.int32, c3.shape, 1)
    out = None
    for b in range(SUBLANES // (2 * h)):
        r = b * 2 * h + pick
        rows = jnp.broadcast_to(c3[:, r:r + 1, :], c3.shape)
        out = rows if out is None else jnp.where(sub >= b * 2 * h, rows, out)
    return out.reshape(chunk, cum.shape[1])


def _cumsum_exact(tri, lf):
    p1 = lf.astype(BF16)
    r1 = lf - p1.astype(F32)
    p2 = r1.astype(BF16)
    p3 = (r1 - p2.astype(F32)).astype(BF16)
    return _dot(tri, p1) + _dot(tri, p2) + _dot(tri, p3)


def _chunk_prep(k, lf, v, tri, reverse, q=None):
    chunk = lf.shape[0]
    cum = _cumsum_exact(tri, lf)
    total = cum[0:1, :] if reverse else cum[chunk - 1:chunk, :]
    prep = dict(decay=jnp.exp(total), v=v, v_t=v.astype(F32).T.astype(BF16),
                k_out=(k.astype(F32) * jnp.exp(total - cum)).astype(BF16))
    if q is None:
        return prep
    prep["q_in"] = (q.astype(F32) * jnp.exp(cum)).astype(BF16)
    pairs = [(q, k)]
    h = 1
    while h <= chunk // 2:
        e = jnp.exp(-jnp.abs(cum - _ref_rows(cum, h, reverse))).astype(BF16)
        pairs.insert(0, (q * e, k * e))
        h *= 2
    prep["pairs"] = pairs
    return prep


def _chunk_apply(st, prep, masks=None):
    st_new = st * prep["decay"] + _dot(prep["v_t"], prep["k_out"])
    if masks is None:
        return st_new, None
    out = _dot_nt(prep["q_in"], st.astype(BF16))
    scores = None
    for level, (qe, ke) in enumerate(prep["pairs"]):
        term = masks[level] * _dot_nt(qe, ke)
        scores = term if scores is None else scores + term
    return st_new, out + _dot(scores.astype(BF16), prep["v"])


def _scan_chunks(n, reverse, prep_fn, apply_fn):
    order = list(range(n - 1, -1, -1) if reverse else range(n))
    prep = prep_fn(order[0])
    for pos, c in enumerate(order):
        nxt = prep_fn(order[pos + 1]) if pos + 1 < n else None
        apply_fn(c, prep)
        prep = nxt


def _scan_state_kernel(k_ref, lf_ref, v_ref, s0_ref, tri_ref, sfin_ref, st_ref, *, chunk, reverse):
    step = pl.program_id(2)

    @pl.when(step == 0)
    def _():
        st_ref[...] = s0_ref[...]

    tri = tri_ref[...]
    state = [st_ref[...]]

    def prep_fn(c):
        rows = slice(c * chunk, (c + 1) * chunk)
        return _chunk_prep(k_ref[rows, :], lf_ref[rows, :], v_ref[rows, :], tri, reverse)

    def apply_fn(c, prep):
        state[0], _ = _chunk_apply(state[0], prep)

    _scan_chunks(k_ref.shape[0] // chunk, reverse, prep_fn, apply_fn)
    st_ref[...] = state[0]

    @pl.when(step == pl.num_programs(2) - 1)
    def _():
        sfin_ref[...] = state[0]


def _scan_out_kernel(*refs, chunk, reverse, final):
    if final:
        q_ref, k_ref, lf_ref, v_ref, s0_ref, tri_ref, masks_ref, of_ref, g_ref, gain_ref, o_ref, st_ref = refs
    else:
        q_ref, k_ref, lf_ref, v_ref, s0_ref, tri_ref, masks_ref, o_ref, st_ref = refs

    @pl.when(pl.program_id(2) == 0)
    def _():
        st_ref[...] = s0_ref[...]

    tri = tri_ref[...]
    masks = masks_ref[...]
    state = [st_ref[...]]

    def prep_fn(c):
        rows = slice(c * chunk, (c + 1) * chunk)
        return _chunk_prep(k_ref[rows, :], lf_ref[rows, :], v_ref[rows, :], tri, reverse, q=q_ref[rows, :])

    def apply_fn(c, prep):
        rows = slice(c * chunk, (c + 1) * chunk)
        state[0], out = _chunk_apply(state[0], prep, masks)
        if final:
            tot = out + of_ref[rows, :]
            ms = jnp.mean(tot * tot, axis=-1, keepdims=True)
            gate = g_ref[rows, :].astype(F32)
            out = tot * lax.rsqrt(ms + EPS) * gain_ref[...] * (gate * jax.nn.sigmoid(gate))
        o_ref[rows, :] = out.astype(o_ref.dtype)

    _scan_chunks(k_ref.shape[0] // chunk, reverse, prep_fn, apply_fn)
    st_ref[...] = state[0]


def _scan_geometry(t, blk_pref=1024, chunk_pref=128):
    chunk = _pick(t, chunk_pref, LANES)
    blk = _pick(t, blk_pref, chunk)
    return chunk, blk


def _scan_state(kf, lf, qig, s0, direction):
    bsz, t, _ = kf.shape
    n_h = s0.shape[1]
    reverse = direction == 1
    chunk, blk = _scan_geometry(t)
    n_blk = t // blk
    _, tri = _scan_masks(chunk, reverse)
    pos = (lambda c: n_blk - 1 - c) if reverse else (lambda c: c)
    kcol = direction * n_h
    return pl.pallas_call(
        functools.partial(_scan_state_kernel, chunk=chunk, reverse=reverse),
        grid=(bsz, n_h, n_blk),
        in_specs=[pl.BlockSpec((None, blk, REC_DK), lambda b, h, c: (b, pos(c), kcol + h)),
                  pl.BlockSpec((None, blk, REC_DK), lambda b, h, c: (b, pos(c), kcol + h)),
                  pl.BlockSpec((None, blk, REC_DK), lambda b, h, c: (b, pos(c), n_h + h)),
                  pl.BlockSpec((None, None, REC_DK, REC_DK), lambda b, h, c: (b, h, 0, 0)),
                  pl.BlockSpec((chunk, chunk), lambda b, h, c: (0, 0))],
        out_specs=pl.BlockSpec((None, None, REC_DK, REC_DK), lambda b, h, c: (b, h, 0, 0)),
        out_shape=jax.ShapeDtypeStruct(s0.shape, F32),
        scratch_shapes=[pltpu.VMEM((REC_DK, REC_DK), F32)],
        compiler_params=_params(("arbitrary", "arbitrary", "arbitrary")),
        name="hgrn_state_bwd" if reverse else "hgrn_state_fwd",
    )(kf, lf, qig, s0, jnp.asarray(tri, BF16))


def _scan_out(kf, lf, qig, s0, direction, other=None, out_gain=None):
    bsz, t, _ = kf.shape
    n_h = s0.shape[1]
    reverse = direction == 1
    final = other is not None
    chunk, blk = _scan_geometry(t)
    n_blk = t // blk
    masks, tri = _scan_masks(chunk, reverse)
    pos = (lambda c: n_blk - 1 - c) if reverse else (lambda c: c)
    kcol = direction * n_h
    tok = lambda col: pl.BlockSpec((None, blk, REC_DK), lambda b, h, c: (b, pos(c), col + h))
    in_specs = [tok(0), tok(kcol), tok(kcol), tok(n_h),
                pl.BlockSpec((None, None, REC_DK, REC_DK), lambda b, h, c: (b, h, 0, 0)),
                pl.BlockSpec((chunk, chunk), lambda b, h, c: (0, 0)),
                pl.BlockSpec(masks.shape, lambda b, h, c: (0, 0, 0))]
    args = [qig, kf, lf, qig, s0, jnp.asarray(tri, BF16), jnp.asarray(masks)]
    if final:
        in_specs += [tok(0), tok(2 * n_h), pl.BlockSpec((1, REC_DK), lambda b, h, c: (0, 0))]
        args += [other, qig, out_gain.reshape(1, REC_DK)]
    return pl.pallas_call(
        functools.partial(_scan_out_kernel, chunk=chunk, reverse=reverse, final=final),
        grid=(bsz, n_h, n_blk),
        in_specs=in_specs,
        out_specs=tok(0),
        out_shape=jax.ShapeDtypeStruct((bsz, t, n_h * REC_DK), BF16 if final else F32),
        scratch_shapes=[pltpu.VMEM((REC_DK, REC_DK), F32)],
        compiler_params=_params(("arbitrary", "arbitrary", "arbitrary")),
        name="hgrn_scan_bwd" if reverse else "hgrn_scan_fwd",
    )(*args)


def _rope_tables(n_tokens):
    n_rows = n_tokens // GRID_W
    row = jnp.repeat(jnp.arange(n_rows, dtype=F32), GRID_W)
    col = jnp.tile(jnp.arange(GRID_W, dtype=F32), n_rows)
    quarter = HEAD_DIM // 4
    inv_freq = ROPE_THETA ** (-jnp.arange(quarter, dtype=F32) / quarter)
    ang = jnp.concatenate([row[:, None] * inv_freq, col[:, None] * inv_freq], axis=-1)
    cos, sin = jnp.cos(ang), jnp.sin(ang)
    return jnp.concatenate([cos, cos], axis=-1), jnp.concatenate([-sin, sin], axis=-1)


def _conv_ffn(x, gain, shift, scale, gate, w_up, conv_w, conv_b, w_down):
    act = _ffn_up(x, gain, shift, scale, w_up, conv_w, conv_b)
    return _mm_res([act], [w_down], x, gate)


def kernel(x, c, ctx, c_ctx, w_mod, b_mod, norm_gain, attn_w_in, attn_w_out, attn_qk_gain, attn_sink,
           rec_w_in, rec_w_out, rec_out_gain, rec_lb_gamma, ffn_w_up, ffn_conv_w, ffn_conv_b, ffn_w_down):
    bsz, n_lat, d = x.shape
    depth = w_mod.shape[0]
    assert depth == 2 and bsz < SUBLANES, "layer 0 attention, layer 1 recurrence; conditioning rows fit one tile"
    n_heads_rec = d // REC_DK

    cvec = jnp.zeros((SUBLANES, d), F32).at[:bsz].set(c).at[bsz].set(c_ctx)
    mods = _mods(cvec, w_mod, b_mod)

    def lat_mod(layer, k):
        return mods[layer, :bsz, k * d:(k + 1) * d][:, None, :]

    def ctx_mod(layer, k):
        return jnp.broadcast_to(mods[layer, bsz, k * d:(k + 1) * d][None, None, :], (bsz, 1, d))

    w_up = ffn_w_up.astype(BF16)
    w_down = ffn_w_down.astype(BF16)

    aq = N_Q_HEADS * HEAD_DIM
    akv = N_KV_HEADS * HEAD_DIM
    w_in = attn_w_in[0]
    qa, ka, va, qb, kb, vb = jnp.split(w_in, np.cumsum([aq, akv, akv, aq, akv])[:5].tolist(), axis=1)
    w_qk = jnp.concatenate([qa, qb, ka, kb], axis=1).astype(BF16)
    w_v = jnp.concatenate([va, vb], axis=1).astype(BF16)
    qkg = attn_qk_gain[0]
    q_scale = ATTN_SCALE * LOG2E
    head_gain = jnp.concatenate([jnp.tile(qkg[0] * q_scale, N_Q_HEADS), jnp.tile(qkg[2] * q_scale, N_Q_HEADS),
                                 jnp.tile(qkg[1], N_KV_HEADS), jnp.tile(qkg[3], N_KV_HEADS)])[None, :]
    cos, sin = _rope_tables(n_lat)
    g0 = norm_gain[0, 0]
    qk_l = _proj_qk(x, g0, lat_mod(0, 0), lat_mod(0, 1), w_qk, head_gain, cos, sin)
    v_l = _proj_plain(x, g0, lat_mod(0, 0), lat_mod(0, 1), w_v)
    qk_c = _proj_qk(ctx, g0, ctx_mod(0, 0), ctx_mod(0, 1), w_qk, head_gain)
    v_c = _proj_plain(ctx, g0, ctx_mod(0, 0), ctx_mod(0, 1), w_v)

    ka_cols = slice(2 * aq, 2 * aq + akv)
    k_all = jnp.concatenate([qk_c[:, :, ka_cols], qk_l[:, :, ka_cols]], axis=1)
    v_all = jnp.concatenate([v_c[:, :, :akv], v_l[:, :, :akv]], axis=1)
    ones = jnp.ones(v_all.shape[:2] + (HEAD_DIM,), BF16)
    v_ext = jnp.concatenate([v_all[:, :, :HEAD_DIM], ones, v_all[:, :, HEAD_DIM:], ones], axis=-1)
    sink = attn_sink[0].astype(F32)
    o_a = _attn_a(qk_l, k_all, v_ext)
    o_b = _attn_b(qk_l, v_l, qk_c, v_c, sink)
    o_c = _attn_ctx(qk_c, v_c, jnp.concatenate([jnp.full((N_Q_HEADS,), -jnp.inf, F32), sink]))

    w_out = attn_w_out[0].astype(BF16)
    x = _mm_res([o_a, o_b], [w_out[:aq], w_out[aq:]], x, lat_mod(0, 2))
    ctx = _mm_res([o_c], [w_out], ctx, ctx_mod(0, 2))
    g1 = norm_gain[0, 1]
    x = _conv_ffn(x, g1, lat_mod(0, 3), lat_mod(0, 4), lat_mod(0, 5), w_up[0], ffn_conv_w[0], ffn_conv_b[0], w_down[0])
    ctx = _conv_ffn(ctx, g1, ctx_mod(0, 3), ctx_mod(0, 4), ctx_mod(0, 5), w_up[0], ffn_conv_w[0], ffn_conv_b[0],
                    w_down[0])

    lb_soft = jax.nn.softmax(rec_lb_gamma.astype(F32), axis=1)
    lower = (jnp.cumsum(lb_soft, axis=1) - lb_soft[:, :1])[:, 1]
    rk = n_heads_rec * REC_DK
    w_q, w_ff, w_fb, w_i, w_g = jnp.split(rec_w_in[0], [rk, 2 * rk, 3 * rk, 3 * rk + d], axis=1)
    w_qig = jnp.concatenate([w_q, w_i, w_g], axis=1).astype(BF16)
    w_f = jnp.concatenate([w_ff, w_fb], axis=1).astype(BF16)
    lb = lower.reshape(1, 2 * rk)
    g0 = norm_gain[1, 0]
    qig_l = _proj_plain(x, g0, lat_mod(1, 0), lat_mod(1, 1), w_qig)
    lf_l, kf_l = _proj_gate(x, g0, lat_mod(1, 0), lat_mod(1, 1), w_f, lb)
    qig_c = _proj_plain(ctx, g0, ctx_mod(1, 0), ctx_mod(1, 1), w_qig)
    lf_c, kf_c = _proj_gate(ctx, g0, ctx_mod(1, 0), ctx_mod(1, 1), w_f, lb)

    s_zero = jnp.zeros((bsz, n_heads_rec, REC_DK, REC_DK), F32)
    s_fwd = _scan_state(kf_c, lf_c, qig_c, s_zero, 0)
    s_bwd = _scan_state(kf_c, lf_c, qig_c, s_zero, 1)
    o_f = _scan_out(kf_l, lf_l, qig_l, s_fwd, 0)
    y = _scan_out(kf_l, lf_l, qig_l, s_bwd, 1, other=o_f, out_gain=rec_out_gain[0])
    x = _mm_res([y], [rec_w_out[0].astype(BF16)], x, lat_mod(1, 2))
    g1 = norm_gain[1, 1]
    x = _conv_ffn(x, g1, lat_mod(1, 3), lat_mod(1, 4), lat_mod(1, 5), w_up[1], ffn_conv_w[1], ffn_conv_b[1], w_down[1])
    return x
```

```python
import functools

import numpy as np
import jax
import jax.numpy as jnp
from jax import lax
from jax.experimental import pallas as pl
from jax.experimental.pallas import tpu as pltpu

F32 = jnp.float32
BF16 = jnp.bfloat16

HEAD_DIM = 128
GRID_W = 64
N_Q_HEADS = 8
N_KV_HEADS = 2
GROUP = N_Q_HEADS // N_KV_HEADS
WINDOW = 128
ROPE_THETA = 10000.0
ATTN_SCALE = HEAD_DIM ** -0.5
NEG_INF = -1e30
LOG2E = 1.4426950408889634
REC_DK = 128
FAST_BASE = 64
MAX_EXP2 = 100.0
N_MOD = 6
EPS = 1e-6

LANES = 128
SUBLANES = 8
BF16_ROWS = 16
MXU_COLS = 256
CHAINS_PER_STAGE = 4
VMEM_LIMIT = 56 * 1024 * 1024
W_TILE_BYTES = 6 * 1024 * 1024


def _pick(n, pref, align):
    if n <= pref:
        return n
    t = (pref // align) * align
    while t >= align:
        if n % t == 0:
            return t
        t -= align
    raise ValueError(f"no tile for {n} (pref {pref}, align {align})")


def _params(sem):
    return pltpu.CompilerParams(dimension_semantics=sem, vmem_limit_bytes=VMEM_LIMIT)


def _dot(a, b):
    return jnp.dot(a, b, preferred_element_type=F32)


def _dot_nt(a, b):
    return lax.dot_general(a, b, (((1,), (1,)), ((), ())), preferred_element_type=F32)


def _mod_kernel(c_ref, w_ref, b_ref, o_ref):
    cv = c_ref[...]
    o_ref[...] = _dot(cv * jax.nn.sigmoid(cv), w_ref[...]) + b_ref[...]


def _mods(cvec, w_mod, b_mod):
    depth, d, n = w_mod.shape
    tn = _pick(n, 1024, LANES)
    return pl.pallas_call(
        _mod_kernel,
        grid=(depth, n // tn),
        in_specs=[pl.BlockSpec((SUBLANES, d), lambda l, j: (0, 0)),
                  pl.BlockSpec((None, d, tn), lambda l, j: (l, 0, j)),
                  pl.BlockSpec((None, 1, tn), lambda l, j: (l, 0, j))],
        out_specs=pl.BlockSpec((None, SUBLANES, tn), lambda l, j: (l, 0, j)),
        out_shape=jax.ShapeDtypeStruct((depth, SUBLANES, n), F32),
        compiler_params=_params(("arbitrary", "arbitrary")),
        name="adaln_mods",
    )(cvec, w_mod, b_mod.reshape(depth, 1, n))


def _norm_mod(x, g, sh, sc):
    ms = jnp.mean(x * x, axis=-1, keepdims=True)
    return (x * lax.rsqrt(ms + EPS)) * (g * (1.0 + sc)) + sh


def _fill_h(x_ref, g_ref, sh_ref, sc_ref, h_ref):
    @pl.when(pl.program_id(2) == 0)
    def _():
        h_ref[...] = _norm_mod(x_ref[...], g_ref[...], sh_ref[...], sc_ref[...]).astype(BF16)


def _proj_plain_kernel(x_ref, g_ref, sh_ref, sc_ref, w_ref, o_ref, h_ref):
    _fill_h(x_ref, g_ref, sh_ref, sc_ref, h_ref)
    o_ref[...] = _dot(h_ref[...], w_ref[...]).astype(o_ref.dtype)


def _proj_qk_kernel(*refs, rope):
    if rope:
        x_ref, g_ref, sh_ref, sc_ref, w_ref, hg_ref, cos_ref, sin_ref, o_ref, h_ref = refs
    else:
        x_ref, g_ref, sh_ref, sc_ref, w_ref, hg_ref, o_ref, h_ref = refs
    _fill_h(x_ref, g_ref, sh_ref, sc_ref, h_ref)
    h = h_ref[...]
    accs = [_dot(h, w_ref[:, c0:c0 + MXU_COLS]) for c0 in range(0, o_ref.shape[1], MXU_COLS)]
    heads = [acc[:, c:c + HEAD_DIM] for acc in accs for c in range(0, MXU_COLS, HEAD_DIM)]
    cols = [slice(c, c + HEAD_DIM) for c in range(0, o_ref.shape[1], HEAD_DIM)]
    inv = [lax.rsqrt(jnp.mean(x * x, axis=-1, keepdims=True) + EPS) for x in heads]
    ys = [x * r * hg_ref[:, c] for x, r, c in zip(heads, inv, cols)]
    if rope:
        ys = [y * cos_ref[...] + pltpu.roll(y, HEAD_DIM // 2, 1) * sin_ref[...] for y in ys]
    for y, c in zip(ys, cols):
        o_ref[:, c] = y.astype(o_ref.dtype)


def _proj_gate_kernel(x_ref, g_ref, sh_ref, sc_ref, w_ref, lb_ref, lf_ref, k_ref, h_ref):
    _fill_h(x_ref, g_ref, sh_ref, sc_ref, h_ref)
    h = h_ref[...]
    for c0 in range(0, lf_ref.shape[1], MXU_COLS):
        cols = slice(c0, c0 + MXU_COLS)
        lb = lb_ref[:, cols]
        sig = jax.nn.sigmoid(_dot(h, w_ref[:, cols]))
        lf_ref[:, cols] = jnp.log(lb + (1.0 - lb) * sig) * LOG2E
        k_ref[:, cols] = ((1.0 - lb) * (1.0 - sig)).astype(k_ref.dtype)


def _proj_call(kernel, x, gain, shift, scale, w, extras, extra_specs, out_dtypes, tm_pref=1024, tn_pref=1024):
    bsz, t, d = x.shape
    n = w.shape[1]
    tm = _pick(t, tm_pref, BF16_ROWS)
    tn = _pick(n, tn_pref, LANES)
    in_specs = [pl.BlockSpec((None, tm, d), lambda b, i, j: (b, i, 0)),
                pl.BlockSpec((1, d), lambda b, i, j: (0, 0)),
                pl.BlockSpec((None, 1, d), lambda b, i, j: (b, 0, 0)),
                pl.BlockSpec((None, 1, d), lambda b, i, j: (b, 0, 0)),
                pl.BlockSpec((d, tn), lambda b, i, j: (0, j))] + [s(tm, tn) for s in extra_specs]
    out_spec = pl.BlockSpec((None, tm, tn), lambda b, i, j: (b, i, j))
    outs = pl.pallas_call(
        kernel,
        grid=(bsz, t // tm, n // tn),
        in_specs=in_specs,
        out_specs=[out_spec] * len(out_dtypes),
        out_shape=[jax.ShapeDtypeStruct((bsz, t, n), dt) for dt in out_dtypes],
        scratch_shapes=[pltpu.VMEM((tm, d), BF16)],
        compiler_params=_params(("arbitrary", "arbitrary", "arbitrary")),
        name=getattr(kernel, "__name__", "proj").strip("_"),
    )(x, gain.reshape(1, d), shift, scale, w, *extras)
    return outs


def _col_spec(tm, tn):
    return pl.BlockSpec((1, tn), lambda b, i, j: (0, j))


def _pos_spec(tm, tn):
    return pl.BlockSpec((tm, HEAD_DIM), lambda b, i, j: (i, 0))


def _proj_plain(x, gain, shift, scale, w, dtype=BF16):
    return _proj_call(_proj_plain_kernel, x, gain, shift, scale, w, (), (), (dtype,))[0]


def _proj_qk(x, gain, shift, scale, w, head_gain, cos=None, sin=None):
    if cos is None:
        kern = functools.partial(_proj_qk_kernel, rope=False)
        kern.__name__ = "proj_qk_ctx"
        return _proj_call(kern, x, gain, shift, scale, w, (head_gain,), (_col_spec,), (BF16,),
                          tn_pref=5 * MXU_COLS)[0]
    kern = functools.partial(_proj_qk_kernel, rope=True)
    kern.__name__ = "proj_qk_rope"
    return _proj_call(kern, x, gain, shift, scale, w, (head_gain, cos, sin),
                      (_col_spec, _pos_spec, _pos_spec), (BF16,), tn_pref=5 * MXU_COLS)[0]


def _proj_gate(x, gain, shift, scale, w, lb):
    return _proj_call(_proj_gate_kernel, x, gain, shift, scale, w, (lb,), (_col_spec,), (F32, BF16))


def _mm_res_kernel(a_ref, w_ref, r_ref, gt_ref, o_ref):
    o_ref[...] = r_ref[...] + gt_ref[...] * _dot(a_ref[...], w_ref[...])


def _mm2_res_kernel(a1_ref, a2_ref, w1_ref, w2_ref, r_ref, gt_ref, o_ref):
    acc = _dot(a1_ref[...], w1_ref[...]) + _dot(a2_ref[...], w2_ref[...])
    o_ref[...] = r_ref[...] + gt_ref[...] * acc


def _mm_res(acts, ws, res, gate, tm_pref=1024):
    bsz, t, n = res.shape
    tm = _pick(t, tm_pref, BF16_ROWS)
    k_total = sum(w.shape[0] for w in ws)
    tn = _pick(n, max(LANES, W_TILE_BYTES // (2 * k_total)), LANES)
    kernel = _mm_res_kernel if len(acts) == 1 else _mm2_res_kernel
    in_specs = ([pl.BlockSpec((None, tm, a.shape[2]), lambda b, i, j: (b, i, 0)) for a in acts]
                + [pl.BlockSpec((w.shape[0], tn), lambda b, i, j: (0, j)) for w in ws]
                + [pl.BlockSpec((None, tm, tn), lambda b, i, j: (b, i, j)),
                   pl.BlockSpec((None, 1, tn), lambda b, i, j: (b, 0, j))])
    return pl.pallas_call(
        kernel,
        grid=(bsz, t // tm, n // tn),
        in_specs=in_specs,
        out_specs=pl.BlockSpec((None, tm, tn), lambda b, i, j: (b, i, j)),
        out_shape=jax.ShapeDtypeStruct((bsz, t, n), F32),
        compiler_params=_params(("arbitrary", "arbitrary", "arbitrary")),
        name="mm_res" if len(acts) == 1 else "mm2_res",
    )(*acts, *ws, res, gate)


def _ffn_up_kernel(xp_ref, x_ref, xn_ref, g_ref, sh_ref, sc_ref, wg_ref, wv_ref, cwg_ref, cwv_ref,
                   cbg_ref, cbv_ref, o_ref, h_ref, *, tm, halo):
    i = pl.program_id(1)
    last = pl.num_programs(1) - 1

    @pl.when(pl.program_id(2) == 0)
    def _():
        g, sh, sc = g_ref[...], sh_ref[...], sc_ref[...]
        hp = jnp.where(i > 0, _norm_mod(xp_ref[...], g, sh, sc), 0.0)
        hn = jnp.where(i < last, _norm_mod(xn_ref[...], g, sh, sc), 0.0)
        h_ref[0:halo, :] = hp.astype(BF16)
        h_ref[halo:halo + tm, :] = _norm_mod(x_ref[...], g, sh, sc).astype(BF16)
        h_ref[halo + tm:, :] = hn.astype(BF16)

    h = h_ref[...]
    rows = tm + 2 * halo

    def branch(w_ref, cw_ref, cb_ref):
        u = _dot(h, w_ref[...])
        u_prev = pltpu.roll(u, 1, 0)[halo:halo + tm]
        u_next = pltpu.roll(u, rows - 1, 0)[halo:halo + tm]
        cw = cw_ref[...]
        return u_prev * cw[0:1] + u[halo:halo + tm] * cw[1:2] + u_next * cw[2:3] + cb_ref[...]

    yg = branch(wg_ref, cwg_ref, cbg_ref)
    yv = branch(wv_ref, cwv_ref, cbv_ref)
    o_ref[...] = (yg * jax.nn.sigmoid(yg) * yv).astype(o_ref.dtype)


def _ffn_up(x, gain, shift, scale, w_up, conv_w, conv_b, tm_pref=1024, tn_pref=512):
    bsz, t, d = x.shape
    d_ff = w_up.shape[1] // 2
    halo = BF16_ROWS
    tm = _pick(t, tm_pref, halo)
    tn = _pick(d_ff, tn_pref, LANES)
    nj = d_ff // tn
    r = tm // halo
    n_halo = t // halo
    kern = functools.partial(_ffn_up_kernel, tm=tm, halo=halo)
    conv_b = conv_b.reshape(1, 2 * d_ff)
    vec = lambda b, i, j: (b, 0, 0)
    return pl.pallas_call(
        kern,
        grid=(bsz, t // tm, nj),
        in_specs=[pl.BlockSpec((None, halo, d), lambda b, i, j: (b, jnp.maximum(i * r - 1, 0), 0)),
                  pl.BlockSpec((None, tm, d), lambda b, i, j: (b, i, 0)),
                  pl.BlockSpec((None, halo, d), lambda b, i, j: (b, jnp.minimum((i + 1) * r, n_halo - 1), 0)),
                  pl.BlockSpec((1, d), lambda b, i, j: (0, 0)),
                  pl.BlockSpec((None, 1, d), vec),
                  pl.BlockSpec((None, 1, d), vec),
                  pl.BlockSpec((d, tn), lambda b, i, j: (0, j)),
                  pl.BlockSpec((d, tn), lambda b, i, j: (0, j + nj)),
                  pl.BlockSpec((conv_w.shape[0], tn), lambda b, i, j: (0, j)),
                  pl.BlockSpec((conv_w.shape[0], tn), lambda b, i, j: (0, j + nj)),
                  pl.BlockSpec((1, tn), lambda b, i, j: (0, j)),
                  pl.BlockSpec((1, tn), lambda b, i, j: (0, j + nj))],
        out_specs=pl.BlockSpec((None, tm, tn), lambda b, i, j: (b, i, j)),
        out_shape=jax.ShapeDtypeStruct((bsz, t, d_ff), BF16),
        scratch_shapes=[pltpu.VMEM((tm + 2 * halo, d), BF16)],
        compiler_params=_params(("arbitrary", "arbitrary", "arbitrary")),
        name="ffn_up_conv",
    )(x, x, x, gain.reshape(1, d), shift, scale, w_up, w_up, conv_w, conv_w, conv_b, conv_b)


def _attn_a_kernel(q_ref, k_ref, v_ref, o_ref, m_ref, acc_ref, *, tq, rs):
    step = pl.program_id(3)

    @pl.when(step == 0)
    def _():
        m_ref[...] = jnp.full(m_ref.shape, -jnp.inf, F32)
        acc_ref[...] = jnp.zeros(acc_ref.shape, F32)

    k = k_ref[...]
    v = v_ref[...]
    n_chunk = k.shape[0] // LANES
    chains = [(g, r) for g in range(GROUP) for r in range(tq // rs)]
    for start in range(0, len(chains), CHAINS_PER_STAGE):
        group = chains[start:start + CHAINS_PER_STAGE]
        rows = [slice(g * tq + r * rs, g * tq + (r + 1) * rs) for g, r in group]
        scores = [_dot_nt(q_ref[r * rs:(r + 1) * rs, g * HEAD_DIM:(g + 1) * HEAD_DIM], k) for g, r in group]
        chunks = [[s[:, c * LANES:(c + 1) * LANES] for c in range(n_chunk)] for s in scores]
        m_prev = [m_ref[rw, :] for rw in rows]
        m_new = [jnp.maximum(mp, jnp.max(functools.reduce(jnp.maximum, ch), axis=-1, keepdims=True))
                 for mp, ch in zip(m_prev, chunks)]
        probs = [jnp.concatenate([jnp.exp2(c - mn) for c in ch], axis=1).astype(BF16) for mn, ch in zip(m_new, chunks)]
        pv = [_dot(p, v) for p in probs]
        for rw, mp, mn, upd in zip(rows, m_prev, m_new, pv):
            alpha = jnp.exp2(mp - mn)
            acc_ref[rw, :] = jnp.concatenate([alpha, alpha], axis=1) * acc_ref[rw, :] + upd
            m_ref[rw, :] = mn

    @pl.when(step == pl.num_programs(3) - 1)
    def _():
        for g in range(GROUP):
            acc = acc_ref[g * tq:(g + 1) * tq, :]
            o_ref[:, g * HEAD_DIM:(g + 1) * HEAD_DIM] = (acc[:, :HEAD_DIM] / acc[:, HEAD_DIM:]).astype(o_ref.dtype)


def _attn_a(qk, k_all, v_ext, tq_pref=1024, tk_pref=3328, rs_pref=256):
    bsz, t, _ = qk.shape
    s = k_all.shape[1]
    tq = _pick(t, tq_pref, BF16_ROWS)
    rs = _pick(tq, rs_pref, BF16_ROWS)
    tk = _pick(s, tk_pref, LANES)
    gw = GROUP * HEAD_DIM
    m = GROUP * tq
    return pl.pallas_call(
        functools.partial(_attn_a_kernel, tq=tq, rs=rs),
        grid=(bsz, N_KV_HEADS, t // tq, s // tk),
        in_specs=[pl.BlockSpec((None, tq, gw), lambda b, h, i, j: (b, i, h)),
                  pl.BlockSpec((None, tk, HEAD_DIM), lambda b, h, i, j: (b, j, h)),
                  pl.BlockSpec((None, tk, 2 * HEAD_DIM), lambda b, h, i, j: (b, j, h))],
        out_specs=pl.BlockSpec((None, tq, gw), lambda b, h, i, j: (b, i, h)),
        out_shape=jax.ShapeDtypeStruct((bsz, t, N_KV_HEADS * gw), BF16),
        scratch_shapes=[pltpu.VMEM((m, HEAD_DIM), F32), pltpu.VMEM((m, 2 * HEAD_DIM), F32)],
        compiler_params=_params(("arbitrary", "arbitrary", "arbitrary", "arbitrary")),
        name="attn_global",
    )(qk, k_all, v_ext)


def _softmax_out(s, sink, v):
    sink = sink * LOG2E
    m = jnp.maximum(jnp.max(s, axis=-1, keepdims=True), sink)
    p = jnp.exp2(s - m)
    den = jnp.sum(p, axis=-1, keepdims=True) + jnp.exp2(sink - m)
    return _dot(p.astype(BF16), v) / den


def _attn_b_kernel(sink_ref, q_ref, kc_ref, vc_ref, kp_ref, km_ref, kn_ref, vp_ref, vm_ref, vn_ref, o_ref,
                   kcat_ref, vcat_ref, *, tq, n_ctx, n_lat):
    kv = pl.program_id(1)
    start = pl.program_id(2) * tq
    w = WINDOW
    for dst, parts in ((kcat_ref, (kc_ref, kp_ref, km_ref, kn_ref)), (vcat_ref, (vc_ref, vp_ref, vm_ref, vn_ref))):
        off = 0
        for part in parts:
            dst[off:off + part.shape[0], :] = part[...]
            off += part.shape[0]
    nk = n_ctx + tq + 2 * w
    row = lax.broadcasted_iota(jnp.int32, (tq, nk), 0)
    col = lax.broadcasted_iota(jnp.int32, (tq, nk), 1)
    rel = col - n_ctx - w - row
    kpos = start + col - n_ctx - w
    valid = (col < n_ctx) | ((jnp.abs(rel) <= w) & (kpos >= 0) & (kpos < n_lat))
    kcat = kcat_ref[...]
    vcat = vcat_ref[...]
    heads = range(GROUP)
    cols = [slice(g * HEAD_DIM, (g + 1) * HEAD_DIM) for g in heads]
    scores = [jnp.where(valid, _dot_nt(q_ref[:, cols[g]], kcat), NEG_INF) for g in heads]
    sinks = [sink_ref[kv * GROUP + g] * LOG2E for g in heads]
    tops = [jnp.maximum(jnp.max(scores[g], axis=-1, keepdims=True), sinks[g]) for g in heads]
    probs = [jnp.exp2(scores[g] - tops[g]) for g in heads]
    dens = [jnp.sum(probs[g], axis=-1, keepdims=True) + jnp.exp2(sinks[g] - tops[g]) for g in heads]
    outs = [_dot(probs[g].astype(BF16), vcat) for g in heads]
    for g in heads:
        o_ref[:, cols[g]] = (outs[g] / dens[g]).astype(o_ref.dtype)


def _attn_b(qk, v, qk_c, v_c, sink, tq_pref=256):
    bsz, t, _ = qk.shape
    n_ctx = qk_c.shape[1]
    tq = _pick(t, tq_pref, WINDOW)
    r = tq // WINDOW
    n_w = t // WINDOW
    gw = GROUP * HEAD_DIM
    kb, vb = 18, 2
    prev = lambda c: (lambda b, h, i: (b, jnp.maximum(i * r - 1, 0), c + h))
    main = lambda c: (lambda b, h, i: (b, i, c + h))
    nxt = lambda c: (lambda b, h, i: (b, jnp.minimum((i + 1) * r, n_w - 1), c + h))
    nk = n_ctx + tq + 2 * WINDOW
    return pl.pallas_call(
        functools.partial(_attn_b_kernel, tq=tq, n_ctx=n_ctx, n_lat=t),
        grid=(bsz, N_KV_HEADS, t // tq),
        in_specs=[pl.BlockSpec(memory_space=pltpu.SMEM),
                  pl.BlockSpec((None, tq, gw), lambda b, h, i: (b, i, 2 + h)),
                  pl.BlockSpec((None, n_ctx, HEAD_DIM), lambda b, h, i: (b, 0, kb + h)),
                  pl.BlockSpec((None, n_ctx, HEAD_DIM), lambda b, h, i: (b, 0, vb + h)),
                  pl.BlockSpec((None, WINDOW, HEAD_DIM), prev(kb)),
                  pl.BlockSpec((None, tq, HEAD_DIM), main(kb)),
                  pl.BlockSpec((None, WINDOW, HEAD_DIM), nxt(kb)),
                  pl.BlockSpec((None, WINDOW, HEAD_DIM), prev(vb)),
                  pl.BlockSpec((None, tq, HEAD_DIM), main(vb)),
                  pl.BlockSpec((None, WINDOW, HEAD_DIM), nxt(vb))],
        out_specs=pl.BlockSpec((None, tq, gw), lambda b, h, i: (b, i, h)),
        out_shape=jax.ShapeDtypeStruct((bsz, t, N_KV_HEADS * gw), BF16),
        scratch_shapes=[pltpu.VMEM((nk, HEAD_DIM), BF16), pltpu.VMEM((nk, HEAD_DIM), BF16)],
        compiler_params=_params(("arbitrary", "arbitrary", "arbitrary")),
        name="attn_window",
    )(sink, qk, qk_c, v_c, qk, qk, qk, v, v, v)


def _attn_ctx_kernel(sink_ref, q_ref, k_ref, v_ref, o_ref):
    grp = pl.program_id(1)
    k = k_ref[...]
    v = v_ref[...]
    for g in range(GROUP):
        cols = slice(g * HEAD_DIM, (g + 1) * HEAD_DIM)
        s = _dot_nt(q_ref[:, cols], k)
        o_ref[:, cols] = _softmax_out(s, sink_ref[grp * GROUP + g], v).astype(o_ref.dtype)


def _attn_ctx(qk_c, v_c, sink_all):
    bsz, n_ctx, _ = qk_c.shape
    gw = GROUP * HEAD_DIM
    n_grp = 2 * N_KV_HEADS
    return pl.pallas_call(
        _attn_ctx_kernel,
        grid=(bsz, n_grp),
        in_specs=[pl.BlockSpec(memory_space=pltpu.SMEM),
                  pl.BlockSpec((None, n_ctx, gw), lambda b, h: (b, 0, h)),
                  pl.BlockSpec((None, n_ctx, HEAD_DIM), lambda b, h: (b, 0, 16 + h)),
                  pl.BlockSpec((None, n_ctx, HEAD_DIM), lambda b, h: (b, 0, h))],
        out_specs=pl.BlockSpec((None, n_ctx, gw), lambda b, h: (b, 0, h)),
        out_shape=jax.ShapeDtypeStruct((bsz, n_ctx, n_grp * gw), BF16),
        compiler_params=_params(("arbitrary", "arbitrary")),
        name="attn_ctx",
    )(sink_all, qk_c, qk_c, v_c)


def _scan_consts(chunk, reverse, base):
    idx = np.arange(chunk)
    t = idx[:, None]
    s = idx[None, :]
    tri = ((s >= t) if reverse else (s <= t)).astype(np.float32)
    blocks = [tri]
    masks = []
    h = chunk // 2
    while h >= base:
        same = (t // (2 * h)) == (s // (2 * h))
        t_hi = (t // h) % 2 == 1
        s_hi = (s // h) % 2 == 1
        masks.append(same & (~t_hi & s_hi if reverse else t_hi & ~s_hi))
        ref = (idx // (2 * h)) * (2 * h) + (h if reverse else h - 1)
        blocks.append(tri - tri[ref, :])
        h //= 2
    same = (t // base) == (s // base)
    masks.append(same & ((s >= t) if reverse else (s <= t)))
    if base > 1:
        blocks.append(tri * same)
    dist = np.concatenate(blocks, axis=0)
    return np.stack(masks).astype(np.float32), np.concatenate([dist, dist], axis=1)


def _block_prep(order, rows, dist, reverse, base, k_ref, lf_ref, v_ref, q_ref=None, masks=None):
    chunk = rows(order[0]).stop - rows(order[0]).start
    n_level = (chunk // base).bit_length() - 1
    dists = {}
    for pos in range(0, len(order), 2):
        pair = order[pos:pos + 2]
        cols = []
        for c in pair:
            lf = lf_ref[rows(c), :]
            hi = lf.astype(BF16)
            cols.append(jnp.concatenate([hi, (lf - hi.astype(F32)).astype(BF16)], axis=0))
        dd = _dot(dist, cols[0] if len(cols) == 1 else jnp.concatenate(cols, axis=1))
        for i, c in enumerate(pair):
            dists[c] = dd[:, i * REC_DK:(i + 1) * REC_DK]
    preps = {}
    for c in order:
        d = dists[c]
        cum = d[0:chunk]
        total = cum[0:1, :] if reverse else cum[chunk - 1:chunk, :]
        k = k_ref[rows(c), :]
        prep = dict(decay=jnp.exp2(total), k_out=k * jnp.exp2(total - cum).astype(BF16))
        if q_ref is not None:
            q = q_ref[rows(c), :]
            prep["q_in"] = q * jnp.exp2(cum).astype(BF16)
            pairs = []
            for level in range(n_level):
                e = jnp.exp2(-jnp.abs(d[(level + 1) * chunk:(level + 2) * chunk])).astype(BF16)
                pairs.append((q * e, k * e))
            if base == 1:
                pairs.append((q, k))
            else:
                since = d[(n_level + 1) * chunk:(n_level + 2) * chunk]
                pairs.append((q * jnp.exp2(since).astype(BF16), k * jnp.exp2(-since).astype(BF16)))
            prep["pairs"] = pairs
        preps[c] = prep
    for c in order:
        prep = preps[c]
        prep["kv"] = _dot(v_ref[rows(c), :].astype(F32).T.astype(BF16), prep.pop("k_out"))
        if q_ref is not None:
            prep["terms"] = [_dot_nt(qe, ke) for qe, ke in prep.pop("pairs")]
    if q_ref is not None:
        for c in order:
            terms = preps[c].pop("terms")
            scores = masks[0] * terms[0]
            for level in range(1, len(terms)):
                scores = scores + masks[level] * terms[level]
            preps[c]["scores"] = scores.astype(BF16)
        for c in order:
            preps[c]["intra"] = _dot(preps[c].pop("scores"), v_ref[rows(c), :])
    return preps


def _chunk_apply(st, prep):
    st_new = st * prep["decay"] + prep["kv"]
    if "intra" not in prep:
        return st_new, None
    return st_new, _dot_nt(prep["q_in"], st.astype(BF16)) + prep["intra"]


def _scan_order(n, reverse):
    return list(range(n - 1, -1, -1) if reverse else range(n))


def _scan_state_kernel(k_ref, lf_ref, v_ref, s0_ref, dist_ref, sfin_ref, st_ref, *, chunk, reverse):
    step = pl.program_id(2)

    @pl.when(step == 0)
    def _():
        st_ref[...] = s0_ref[...]

    rows = lambda c: slice(c * chunk, (c + 1) * chunk)
    order = _scan_order(k_ref.shape[0] // chunk, reverse)
    preps = _block_prep(order, rows, dist_ref[...], reverse, 1, k_ref, lf_ref, v_ref)
    st = st_ref[...]
    for c in order:
        st, _ = _chunk_apply(st, preps[c])
    st_ref[...] = st

    @pl.when(step == pl.num_programs(2) - 1)
    def _():
        sfin_ref[...] = st


def _scan_out_kernel(*refs, chunk, reverse, final, base):
    if final:
        q_ref, k_ref, lf_ref, v_ref, s0_ref, dist_ref, masks_ref, of_ref, g_ref, gain_ref, o_ref, st_ref = refs
    else:
        q_ref, k_ref, lf_ref, v_ref, s0_ref, dist_ref, masks_ref, o_ref, st_ref = refs

    @pl.when(pl.program_id(2) == 0)
    def _():
        st_ref[...] = s0_ref[...]

    rows = lambda c: slice(c * chunk, (c + 1) * chunk)
    order = _scan_order(k_ref.shape[0] // chunk, reverse)
    preps = _block_prep(order, rows, dist_ref[...], reverse, base, k_ref, lf_ref, v_ref, q_ref, masks_ref[...])
    st = st_ref[...]
    for c in order:
        st, out = _chunk_apply(st, preps[c])
        if final:
            tot = out + of_ref[rows(c), :]
            ms = jnp.mean(tot * tot, axis=-1, keepdims=True)
            gate = g_ref[rows(c), :].astype(F32)
            out = tot * lax.rsqrt(ms + EPS) * gain_ref[...] * (gate * jax.nn.sigmoid(gate))
        o_ref[rows(c), :] = out.astype(o_ref.dtype)
    st_ref[...] = st


def _scan_geometry(t, blk_pref=2048, chunk_pref=128):
    chunk = _pick(t, chunk_pref, LANES)
    blk = _pick(t, blk_pref, chunk)
    return chunk, blk


def _scan_state(kf, lf, qig, s0, direction):
    bsz, t, _ = kf.shape
    n_h = s0.shape[1]
    reverse = direction == 1
    chunk, blk = _scan_geometry(t)
    n_blk = t // blk
    dist = _scan_consts(chunk, reverse, 1)[1][:chunk]
    pos = (lambda c: n_blk - 1 - c) if reverse else (lambda c: c)
    kcol = direction * n_h
    return pl.pallas_call(
        functools.partial(_scan_state_kernel, chunk=chunk, reverse=reverse),
        grid=(bsz, n_h, n_blk),
        in_specs=[pl.BlockSpec((None, blk, REC_DK), lambda b, h, c: (b, pos(c), kcol + h)),
                  pl.BlockSpec((None, blk, REC_DK), lambda b, h, c: (b, pos(c), kcol + h)),
                  pl.BlockSpec((None, blk, REC_DK), lambda b, h, c: (b, pos(c), n_h + h)),
                  pl.BlockSpec((None, None, REC_DK, REC_DK), lambda b, h, c: (b, h, 0, 0)),
                  pl.BlockSpec(dist.shape, lambda b, h, c: (0, 0))],
        out_specs=pl.BlockSpec((None, None, REC_DK, REC_DK), lambda b, h, c: (b, h, 0, 0)),
        out_shape=jax.ShapeDtypeStruct(s0.shape, F32),
        scratch_shapes=[pltpu.VMEM((REC_DK, REC_DK), F32)],
        compiler_params=_params(("arbitrary", "arbitrary", "arbitrary")),
        name="hgrn_state_bwd" if reverse else "hgrn_state_fwd",
    )(kf, lf, qig, s0, jnp.asarray(dist, BF16))


def _scan_out(kf, lf, qig, s0, direction, base, other=None, out_gain=None):
    bsz, t, _ = kf.shape
    n_h = s0.shape[1]
    reverse = direction == 1
    final = other is not None
    chunk, blk = _scan_geometry(t)
    n_blk = t // blk
    masks, dist = _scan_consts(chunk, reverse, base)
    pos = (lambda c: n_blk - 1 - c) if reverse else (lambda c: c)
    kcol = direction * n_h
    tok = lambda col: pl.BlockSpec((None, blk, REC_DK), lambda b, h, c: (b, pos(c), col + h))
    in_specs = [tok(0), tok(kcol), tok(kcol), tok(n_h),
                pl.BlockSpec((None, None, REC_DK, REC_DK), lambda b, h, c: (b, h, 0, 0)),
                pl.BlockSpec(dist.shape, lambda b, h, c: (0, 0)),
                pl.BlockSpec(masks.shape, lambda b, h, c: (0, 0, 0))]
    args = [qig, kf, lf, qig, s0, jnp.asarray(dist, BF16), jnp.asarray(masks)]
    if final:
        in_specs += [tok(0), tok(2 * n_h), pl.BlockSpec((1, REC_DK), lambda b, h, c: (0, 0))]
        args += [other, qig, out_gain.reshape(1, REC_DK)]
    return pl.pallas_call(
        functools.partial(_scan_out_kernel, chunk=chunk, reverse=reverse, final=final, base=base),
        grid=(bsz, n_h, n_blk),
        in_specs=in_specs,
        out_specs=tok(0),
        out_shape=jax.ShapeDtypeStruct((bsz, t, n_h * REC_DK), BF16 if final else F32),
        scratch_shapes=[pltpu.VMEM((REC_DK, REC_DK), F32)],
        compiler_params=_params(("arbitrary", "arbitrary", "arbitrary")),
        name="hgrn_scan_bwd" if reverse else "hgrn_scan_fwd",
    )(*args)


def _rope_tables(n_tokens):
    n_rows = n_tokens // GRID_W
    row = jnp.repeat(jnp.arange(n_rows, dtype=F32), GRID_W)
    col = jnp.tile(jnp.arange(GRID_W, dtype=F32), n_rows)
    quarter = HEAD_DIM // 4
    inv_freq = ROPE_THETA ** (-jnp.arange(quarter, dtype=F32) / quarter)
    ang = jnp.concatenate([row[:, None] * inv_freq, col[:, None] * inv_freq], axis=-1)
    cos, sin = jnp.cos(ang), jnp.sin(ang)
    return jnp.concatenate([cos, cos], axis=-1), jnp.concatenate([-sin, sin], axis=-1)


def _conv_ffn(x, gain, shift, scale, gate, w_up, conv_w, conv_b, w_down):
    act = _ffn_up(x, gain, shift, scale, w_up, conv_w, conv_b)
    return _mm_res([act], [w_down], x, gate)


def kernel(x, c, ctx, c_ctx, w_mod, b_mod, norm_gain, attn_w_in, attn_w_out, attn_qk_gain, attn_sink,
           rec_w_in, rec_w_out, rec_out_gain, rec_lb_gamma, ffn_w_up, ffn_conv_w, ffn_conv_b, ffn_w_down):
    bsz, n_lat, d = x.shape
    depth = w_mod.shape[0]
    assert depth == 2 and bsz < SUBLANES, "layer 0 attention, layer 1 recurrence; conditioning rows fit one tile"
    n_heads_rec = d // REC_DK

    cvec = jnp.zeros((SUBLANES, d), F32).at[:bsz].set(c).at[bsz].set(c_ctx)
    mods = _mods(cvec, w_mod, b_mod)

    def lat_mod(layer, k):
        return mods[layer, :bsz, k * d:(k + 1) * d][:, None, :]

    def ctx_mod(layer, k):
        return jnp.broadcast_to(mods[layer, bsz, k * d:(k + 1) * d][None, None, :], (bsz, 1, d))

    w_up = ffn_w_up.astype(BF16)
    w_down = ffn_w_down.astype(BF16)

    aq = N_Q_HEADS * HEAD_DIM
    akv = N_KV_HEADS * HEAD_DIM
    w_in = attn_w_in[0]
    qa, ka, va, qb, kb, vb = jnp.split(w_in, np.cumsum([aq, akv, akv, aq, akv])[:5].tolist(), axis=1)
    w_qk = jnp.concatenate([qa, qb, ka, kb], axis=1).astype(BF16)
    w_v = jnp.concatenate([va, vb], axis=1).astype(BF16)
    qkg = attn_qk_gain[0]
    q_scale = ATTN_SCALE * LOG2E
    head_gain = jnp.concatenate([jnp.tile(qkg[0] * q_scale, N_Q_HEADS), jnp.tile(qkg[2] * q_scale, N_Q_HEADS),
                                 jnp.tile(qkg[1], N_KV_HEADS), jnp.tile(qkg[3], N_KV_HEADS)])[None, :]
    cos, sin = _rope_tables(n_lat)
    g0 = norm_gain[0, 0]
    qk_l = _proj_qk(x, g0, lat_mod(0, 0), lat_mod(0, 1), w_qk, head_gain, cos, sin)
    v_l = _proj_plain(x, g0, lat_mod(0, 0), lat_mod(0, 1), w_v)
    qk_c = _proj_qk(ctx, g0, ctx_mod(0, 0), ctx_mod(0, 1), w_qk, head_gain)
    v_c = _proj_plain(ctx, g0, ctx_mod(0, 0), ctx_mod(0, 1), w_v)

    ka_cols = slice(2 * aq, 2 * aq + akv)
    k_all = jnp.concatenate([qk_c[:, :, ka_cols], qk_l[:, :, ka_cols]], axis=1)
    v_all = jnp.concatenate([v_c[:, :, :akv], v_l[:, :, :akv]], axis=1)
    ones = jnp.ones(v_all.shape[:2] + (HEAD_DIM,), BF16)
    v_ext = jnp.concatenate([v_all[:, :, :HEAD_DIM], ones, v_all[:, :, HEAD_DIM:], ones], axis=-1)
    sink = attn_sink[0].astype(F32)
    o_a = _attn_a(qk_l, k_all, v_ext)
    o_b = _attn_b(qk_l, v_l, qk_c, v_c, sink)
    o_c = _attn_ctx(qk_c, v_c, jnp.concatenate([jnp.full((N_Q_HEADS,), -jnp.inf, F32), sink]))

    w_out = attn_w_out[0].astype(BF16)
    x = _mm_res([o_a, o_b], [w_out[:aq], w_out[aq:]], x, lat_mod(0, 2))
    ctx = _mm_res([o_c], [w_out], ctx, ctx_mod(0, 2))
    g1 = norm_gain[0, 1]
    x = _conv_ffn(x, g1, lat_mod(0, 3), lat_mod(0, 4), lat_mod(0, 5), w_up[0], ffn_conv_w[0], ffn_conv_b[0], w_down[0])
    ctx = _conv_ffn(ctx, g1, ctx_mod(0, 3), ctx_mod(0, 4), ctx_mod(0, 5), w_up[0], ffn_conv_w[0], ffn_conv_b[0],
                    w_down[0])

    lb_soft = jax.nn.softmax(rec_lb_gamma.astype(F32), axis=1)
    lower = (jnp.cumsum(lb_soft, axis=1) - lb_soft[:, :1])[:, 1]
    rk = n_heads_rec * REC_DK
    w_q, w_ff, w_fb, w_i, w_g = jnp.split(rec_w_in[0], [rk, 2 * rk, 3 * rk, 3 * rk + d], axis=1)
    w_qig = jnp.concatenate([w_q, w_i, w_g], axis=1).astype(BF16)
    w_f = jnp.concatenate([w_ff, w_fb], axis=1).astype(BF16)
    lb = lower.reshape(1, 2 * rk)
    g0 = norm_gain[1, 0]
    qig_l = _proj_plain(x, g0, lat_mod(1, 0), lat_mod(1, 1), w_qig)
    lf_l, kf_l = _proj_gate(x, g0, lat_mod(1, 0), lat_mod(1, 1), w_f, lb)
    qig_c = _proj_plain(ctx, g0, ctx_mod(1, 0), ctx_mod(1, 1), w_qig)
    lf_c, kf_c = _proj_gate(ctx, g0, ctx_mod(1, 0), ctx_mod(1, 1), w_f, lb)

    s_zero = jnp.zeros((bsz, n_heads_rec, REC_DK, REC_DK), F32)
    s_fwd = _scan_state(kf_c, lf_c, qig_c, s_zero, 0)
    s_bwd = _scan_state(kf_c, lf_c, qig_c, s_zero, 1)

    def scans(base):
        def run(kf, lf, qig, s_f, s_b, gain):
            o_f = _scan_out(kf, lf, qig, s_f, 0, base)
            return _scan_out(kf, lf, qig, s_b, 1, base, other=o_f, out_gain=gain)
        return run

    worst_exp2 = FAST_BASE * jnp.max(-jnp.log2(lower))
    y = lax.cond(worst_exp2 <= MAX_EXP2, scans(FAST_BASE), scans(1),
                 kf_l, lf_l, qig_l, s_fwd, s_bwd, rec_out_gain[0])
    x = _mm_res([y], [rec_w_out[0].astype(BF16)], x, lat_mod(1, 2))
    g1 = norm_gain[1, 1]
    x = _conv_ffn(x, g1, lat_mod(1, 3), lat_mod(1, 4), lat_mod(1, 5), w_up[1], ffn_conv_w[1], ffn_conv_b[1], w_down[1])
    return x
```

```python
import functools

import numpy as np
import jax
import jax.numpy as jnp
from jax import lax
from jax.experimental import pallas as pl
from jax.experimental.pallas import tpu as pltpu

F32 = jnp.float32
BF16 = jnp.bfloat16

HEAD_DIM = 128
GRID_W = 64
N_Q_HEADS = 8
N_KV_HEADS = 2
GROUP = N_Q_HEADS // N_KV_HEADS
WINDOW = 128
ROPE_THETA = 10000.0
ATTN_SCALE = HEAD_DIM ** -0.5
NEG_INF = -1e30
LOG2E = 1.4426950408889634
REC_DK = 128
FAST_BASE = 64
MAX_EXP2 = 100.0
N_MOD = 6
EPS = 1e-6

LANES = 128
SUBLANES = 8
BF16_ROWS = 16
MXU_COLS = 256
CHAINS_PER_STAGE = 4
VMEM_LIMIT = 56 * 1024 * 1024
W_TILE_BYTES = 6 * 1024 * 1024


def _pick(n, pref, align):
    if n <= pref:
        return n
    t = (pref // align) * align
    while t >= align:
        if n % t == 0:
            return t
        t -= align
    raise ValueError(f"no tile for {n} (pref {pref}, align {align})")


def _params(sem):
    return pltpu.CompilerParams(dimension_semantics=sem, vmem_limit_bytes=VMEM_LIMIT)


def _dot(a, b):
    return jnp.dot(a, b, preferred_element_type=F32)


def _dot_nt(a, b):
    return lax.dot_general(a, b, (((1,), (1,)), ((), ())), preferred_element_type=F32)


def _mod_kernel(c_ref, w_ref, b_ref, o_ref):
    cv = c_ref[...]
    o_ref[...] = _dot(cv * jax.nn.sigmoid(cv), w_ref[...]) + b_ref[...]


def _mods(cvec, w_mod, b_mod):
    depth, d, n = w_mod.shape
    tn = _pick(n, 1024, LANES)
    return pl.pallas_call(
        _mod_kernel,
        grid=(depth, n // tn),
        in_specs=[pl.BlockSpec((SUBLANES, d), lambda l, j: (0, 0)),
                  pl.BlockSpec((None, d, tn), lambda l, j: (l, 0, j)),
                  pl.BlockSpec((None, 1, tn), lambda l, j: (l, 0, j))],
        out_specs=pl.BlockSpec((None, SUBLANES, tn), lambda l, j: (l, 0, j)),
        out_shape=jax.ShapeDtypeStruct((depth, SUBLANES, n), F32),
        compiler_params=_params(("arbitrary", "arbitrary")),
        name="adaln_mods",
    )(cvec, w_mod, b_mod.reshape(depth, 1, n))


def _norm_mod(x, g, sh, sc):
    ms = jnp.mean(x * x, axis=-1, keepdims=True)
    return (x * lax.rsqrt(ms + EPS)) * (g * (1.0 + sc)) + sh


def _fill_h(x_ref, g_ref, sh_ref, sc_ref, h_ref):
    @pl.when(pl.program_id(2) == 0)
    def _():
        h_ref[...] = _norm_mod(x_ref[...], g_ref[...], sh_ref[...], sc_ref[...]).astype(BF16)


def _proj_plain_kernel(x_ref, g_ref, sh_ref, sc_ref, w_ref, o_ref, h_ref):
    _fill_h(x_ref, g_ref, sh_ref, sc_ref, h_ref)
    o_ref[...] = _dot(h_ref[...], w_ref[...]).astype(o_ref.dtype)


def _proj_qk_kernel(*refs, rope):
    if rope:
        x_ref, g_ref, sh_ref, sc_ref, w_ref, hg_ref, cos_ref, sin_ref, o_ref, h_ref = refs
    else:
        x_ref, g_ref, sh_ref, sc_ref, w_ref, hg_ref, o_ref, h_ref = refs
    _fill_h(x_ref, g_ref, sh_ref, sc_ref, h_ref)
    h = h_ref[...]
    accs = [_dot(h, w_ref[:, c0:c0 + MXU_COLS]) for c0 in range(0, o_ref.shape[1], MXU_COLS)]
    heads = [acc[:, c:c + HEAD_DIM] for acc in accs for c in range(0, MXU_COLS, HEAD_DIM)]
    cols = [slice(c, c + HEAD_DIM) for c in range(0, o_ref.shape[1], HEAD_DIM)]
    inv = [lax.rsqrt(jnp.mean(x * x, axis=-1, keepdims=True) + EPS) for x in heads]
    ys = [x * r * hg_ref[:, c] for x, r, c in zip(heads, inv, cols)]
    if rope:
        ys = [y * cos_ref[...] + pltpu.roll(y, HEAD_DIM // 2, 1) * sin_ref[...] for y in ys]
    for y, c in zip(ys, cols):
        o_ref[:, c] = y.astype(o_ref.dtype)


def _proj_gate_kernel(x_ref, g_ref, sh_ref, sc_ref, w_ref, lb_ref, lf_ref, k_ref, h_ref):
    _fill_h(x_ref, g_ref, sh_ref, sc_ref, h_ref)
    h = h_ref[...]
    blocks = [slice(c0, c0 + MXU_COLS) for c0 in range(0, lf_ref.shape[1], MXU_COLS)]
    sigs = [jax.nn.sigmoid(_dot(h, w_ref[:, c])) for c in blocks]
    for c, sig in zip(blocks, sigs):
        lb = lb_ref[:, c]
        lf_ref[:, c] = jnp.log2(lb + (1.0 - lb) * sig)
        k_ref[:, c] = ((1.0 - lb) * (1.0 - sig)).astype(k_ref.dtype)


def _proj_call(kernel, x, gain, shift, scale, w, extras, extra_specs, out_dtypes, tm_pref=1024, tn_pref=1024):
    bsz, t, d = x.shape
    n = w.shape[1]
    tm = _pick(t, tm_pref, BF16_ROWS)
    tn = _pick(n, tn_pref, LANES)
    in_specs = [pl.BlockSpec((None, tm, d), lambda b, i, j: (b, i, 0)),
                pl.BlockSpec((1, d), lambda b, i, j: (0, 0)),
                pl.BlockSpec((None, 1, d), lambda b, i, j: (b, 0, 0)),
                pl.BlockSpec((None, 1, d), lambda b, i, j: (b, 0, 0)),
                pl.BlockSpec((d, tn), lambda b, i, j: (0, j))] + [s(tm, tn) for s in extra_specs]
    out_spec = pl.BlockSpec((None, tm, tn), lambda b, i, j: (b, i, j))
    outs = pl.pallas_call(
        kernel,
        grid=(bsz, t // tm, n // tn),
        in_specs=in_specs,
        out_specs=[out_spec] * len(out_dtypes),
        out_shape=[jax.ShapeDtypeStruct((bsz, t, n), dt) for dt in out_dtypes],
        scratch_shapes=[pltpu.VMEM((tm, d), BF16)],
        compiler_params=_params(("arbitrary", "arbitrary", "arbitrary")),
        name=getattr(kernel, "__name__", "proj").strip("_"),
    )(x, gain.reshape(1, d), shift, scale, w, *extras)
    return outs


def _col_spec(tm, tn):
    return pl.BlockSpec((1, tn), lambda b, i, j: (0, j))


def _pos_spec(tm, tn):
    return pl.BlockSpec((tm, HEAD_DIM), lambda b, i, j: (i, 0))


def _proj_plain(x, gain, shift, scale, w, dtype=BF16):
    return _proj_call(_proj_plain_kernel, x, gain, shift, scale, w, (), (), (dtype,))[0]


def _proj_qk(x, gain, shift, scale, w, head_gain, cos=None, sin=None):
    if cos is None:
        kern = functools.partial(_proj_qk_kernel, rope=False)
        kern.__name__ = "proj_qk_ctx"
        return _proj_call(kern, x, gain, shift, scale, w, (head_gain,), (_col_spec,), (BF16,),
                          tn_pref=5 * MXU_COLS)[0]
    kern = functools.partial(_proj_qk_kernel, rope=True)
    kern.__name__ = "proj_qk_rope"
    return _proj_call(kern, x, gain, shift, scale, w, (head_gain, cos, sin),
                      (_col_spec, _pos_spec, _pos_spec), (BF16,), tn_pref=5 * MXU_COLS)[0]


def _proj_gate(x, gain, shift, scale, w, lb):
    return _proj_call(_proj_gate_kernel, x, gain, shift, scale, w, (lb,), (_col_spec,), (F32, BF16))


def _mm_res_kernel(a_ref, w_ref, r_ref, gt_ref, o_ref):
    o_ref[...] = r_ref[...] + gt_ref[...] * _dot(a_ref[...], w_ref[...])


def _mm2_res_kernel(a1_ref, a2_ref, w1_ref, w2_ref, r_ref, gt_ref, o_ref):
    acc = _dot(a1_ref[...], w1_ref[...]) + _dot(a2_ref[...], w2_ref[...])
    o_ref[...] = r_ref[...] + gt_ref[...] * acc


def _mm_res(acts, ws, res, gate, tm_pref=1024):
    bsz, t, n = res.shape
    tm = _pick(t, tm_pref, BF16_ROWS)
    k_total = sum(w.shape[0] for w in ws)
    tn = _pick(n, max(LANES, W_TILE_BYTES // (2 * k_total)), LANES)
    kernel = _mm_res_kernel if len(acts) == 1 else _mm2_res_kernel
    in_specs = ([pl.BlockSpec((None, tm, a.shape[2]), lambda b, i, j: (b, i, 0)) for a in acts]
                + [pl.BlockSpec((w.shape[0], tn), lambda b, i, j: (0, j)) for w in ws]
                + [pl.BlockSpec((None, tm, tn), lambda b, i, j: (b, i, j)),
                   pl.BlockSpec((None, 1, tn), lambda b, i, j: (b, 0, j))])
    return pl.pallas_call(
        kernel,
        grid=(bsz, t // tm, n // tn),
        in_specs=in_specs,
        out_specs=pl.BlockSpec((None, tm, tn), lambda b, i, j: (b, i, j)),
        out_shape=jax.ShapeDtypeStruct((bsz, t, n), F32),
        compiler_params=_params(("arbitrary", "arbitrary", "arbitrary")),
        name="mm_res" if len(acts) == 1 else "mm2_res",
    )(*acts, *ws, res, gate)


def _ffn_up_kernel(xp_ref, x_ref, xn_ref, g_ref, sh_ref, sc_ref, wg_ref, wv_ref, cwg_ref, cwv_ref,
                   cbg_ref, cbv_ref, o_ref, h_ref, *, tm, halo):
    i = pl.program_id(1)
    last = pl.num_programs(1) - 1

    @pl.when(pl.program_id(2) == 0)
    def _():
        g, sh, sc = g_ref[...], sh_ref[...], sc_ref[...]
        hp = jnp.where(i > 0, _norm_mod(xp_ref[...], g, sh, sc), 0.0)
        hn = jnp.where(i < last, _norm_mod(xn_ref[...], g, sh, sc), 0.0)
        h_ref[0:halo, :] = hp.astype(BF16)
        h_ref[halo:halo + tm, :] = _norm_mod(x_ref[...], g, sh, sc).astype(BF16)
        h_ref[halo + tm:, :] = hn.astype(BF16)

    h = h_ref[...]
    rows = tm + 2 * halo

    def branch(w_ref, cw_ref, cb_ref):
        u = _dot(h, w_ref[...])
        u_prev = pltpu.roll(u, 1, 0)[halo:halo + tm]
        u_next = pltpu.roll(u, rows - 1, 0)[halo:halo + tm]
        cw = cw_ref[...]
        return u_prev * cw[0:1] + u[halo:halo + tm] * cw[1:2] + u_next * cw[2:3] + cb_ref[...]

    yg = branch(wg_ref, cwg_ref, cbg_ref)
    yv = branch(wv_ref, cwv_ref, cbv_ref)
    o_ref[...] = (yg * jax.nn.sigmoid(yg) * yv).astype(o_ref.dtype)


def _ffn_up(x, gain, shift, scale, w_up, conv_w, conv_b, tm_pref=1024, tn_pref=512):
    bsz, t, d = x.shape
    d_ff = w_up.shape[1] // 2
    halo = BF16_ROWS
    tm = _pick(t, tm_pref, halo)
    tn = _pick(d_ff, tn_pref, LANES)
    nj = d_ff // tn
    r = tm // halo
    n_halo = t // halo
    kern = functools.partial(_ffn_up_kernel, tm=tm, halo=halo)
    conv_b = conv_b.reshape(1, 2 * d_ff)
    vec = lambda b, i, j: (b, 0, 0)
    return pl.pallas_call(
        kern,
        grid=(bsz, t // tm, nj),
        in_specs=[pl.BlockSpec((None, halo, d), lambda b, i, j: (b, jnp.maximum(i * r - 1, 0), 0)),
                  pl.BlockSpec((None, tm, d), lambda b, i, j: (b, i, 0)),
                  pl.BlockSpec((None, halo, d), lambda b, i, j: (b, jnp.minimum((i + 1) * r, n_halo - 1), 0)),
                  pl.BlockSpec((1, d), lambda b, i, j: (0, 0)),
                  pl.BlockSpec((None, 1, d), vec),
                  pl.BlockSpec((None, 1, d), vec),
                  pl.BlockSpec((d, tn), lambda b, i, j: (0, j)),
                  pl.BlockSpec((d, tn), lambda b, i, j: (0, j + nj)),
                  pl.BlockSpec((conv_w.shape[0], tn), lambda b, i, j: (0, j)),
                  pl.BlockSpec((conv_w.shape[0], tn), lambda b, i, j: (0, j + nj)),
                  pl.BlockSpec((1, tn), lambda b, i, j: (0, j)),
                  pl.BlockSpec((1, tn), lambda b, i, j: (0, j + nj))],
        out_specs=pl.BlockSpec((None, tm, tn), lambda b, i, j: (b, i, j)),
        out_shape=jax.ShapeDtypeStruct((bsz, t, d_ff), BF16),
        scratch_shapes=[pltpu.VMEM((tm + 2 * halo, d), BF16)],
        compiler_params=_params(("arbitrary", "arbitrary", "arbitrary")),
        name="ffn_up_conv",
    )(x, x, x, gain.reshape(1, d), shift, scale, w_up, w_up, conv_w, conv_w, conv_b, conv_b)


def _attn_a_kernel(q_ref, k_ref, v_ref, o_ref, m_ref, acc_ref, *, tq, rs):
    step = pl.program_id(3)

    @pl.when(step == 0)
    def _():
        m_ref[...] = jnp.full(m_ref.shape, -jnp.inf, F32)
        acc_ref[...] = jnp.zeros(acc_ref.shape, F32)

    k = k_ref[...]
    v = v_ref[...]
    n_chunk = k.shape[0] // LANES
    chains = [(g, r) for g in range(GROUP) for r in range(tq // rs)]
    for start in range(0, len(chains), CHAINS_PER_STAGE):
        group = chains[start:start + CHAINS_PER_STAGE]
        rows = [slice(g * tq + r * rs, g * tq + (r + 1) * rs) for g, r in group]
        scores = [_dot_nt(q_ref[r * rs:(r + 1) * rs, g * HEAD_DIM:(g + 1) * HEAD_DIM], k) for g, r in group]
        chunks = [[s[:, c * LANES:(c + 1) * LANES] for c in range(n_chunk)] for s in scores]
        m_prev = [m_ref[rw, :] for rw in rows]
        m_new = [jnp.maximum(mp, jnp.max(functools.reduce(jnp.maximum, ch), axis=-1, keepdims=True))
                 for mp, ch in zip(m_prev, chunks)]
        probs = [jnp.concatenate([jnp.exp2(c - mn) for c in ch], axis=1).astype(BF16) for mn, ch in zip(m_new, chunks)]
        pv = [_dot(p, v) for p in probs]
        for rw, mp, mn, upd in zip(rows, m_prev, m_new, pv):
            alpha = jnp.exp2(mp - mn)
            acc_ref[rw, :] = jnp.concatenate([alpha, alpha], axis=1) * acc_ref[rw, :] + upd
            m_ref[rw, :] = mn

    @pl.when(step == pl.num_programs(3) - 1)
    def _():
        for g in range(GROUP):
            acc = acc_ref[g * tq:(g + 1) * tq, :]
            o_ref[:, g * HEAD_DIM:(g + 1) * HEAD_DIM] = (acc[:, :HEAD_DIM] / acc[:, HEAD_DIM:]).astype(o_ref.dtype)


def _attn_a(qk, k_all, v_ext, tq_pref=1024, tk_pref=3328, rs_pref=256):
    bsz, t, _ = qk.shape
    s = k_all.shape[1]
    tq = _pick(t, tq_pref, BF16_ROWS)
    rs = _pick(tq, rs_pref, BF16_ROWS)
    tk = _pick(s, tk_pref, LANES)
    gw = GROUP * HEAD_DIM
    m = GROUP * tq
    return pl.pallas_call(
        functools.partial(_attn_a_kernel, tq=tq, rs=rs),
        grid=(bsz, N_KV_HEADS, t // tq, s // tk),
        in_specs=[pl.BlockSpec((None, tq, gw), lambda b, h, i, j: (b, i, h)),
                  pl.BlockSpec((None, tk, HEAD_DIM), lambda b, h, i, j: (b, j, h)),
                  pl.BlockSpec((None, tk, 2 * HEAD_DIM), lambda b, h, i, j: (b, j, h))],
        out_specs=pl.BlockSpec((None, tq, gw), lambda b, h, i, j: (b, i, h)),
        out_shape=jax.ShapeDtypeStruct((bsz, t, N_KV_HEADS * gw), BF16),
        scratch_shapes=[pltpu.VMEM((m, HEAD_DIM), F32), pltpu.VMEM((m, 2 * HEAD_DIM), F32)],
        compiler_params=_params(("arbitrary", "arbitrary", "arbitrary", "arbitrary")),
        name="attn_global",
    )(qk, k_all, v_ext)


def _softmax_out(s, sink, v):
    sink = sink * LOG2E
    m = jnp.maximum(jnp.max(s, axis=-1, keepdims=True), sink)
    p = jnp.exp2(s - m)
    den = jnp.sum(p, axis=-1, keepdims=True) + jnp.exp2(sink - m)
    return _dot(p.astype(BF16), v) / den


def _attn_b_kernel(sink_ref, bias_ref, q_ref, kc_ref, vc_ref, kp_ref, km_ref, kn_ref, vp_ref, vm_ref, vn_ref,
                   o_ref, kcat_ref, vcat_ref):
    kv = pl.program_id(1)
    for dst, parts in ((kcat_ref, (kc_ref, kp_ref, km_ref, kn_ref)), (vcat_ref, (vc_ref, vp_ref, vm_ref, vn_ref))):
        off = 0
        for part in parts:
            dst[off:off + part.shape[0], :] = part[...]
            off += part.shape[0]
    kcat = kcat_ref[...]
    vcat = vcat_ref[...]
    bias = bias_ref[...]
    heads = range(GROUP)
    cols = [slice(g * HEAD_DIM, (g + 1) * HEAD_DIM) for g in heads]
    scores = [_dot_nt(q_ref[:, cols[g]], kcat) + bias for g in heads]
    sinks = [sink_ref[kv * GROUP + g] * LOG2E for g in heads]
    tops = [jnp.maximum(jnp.max(scores[g], axis=-1, keepdims=True), sinks[g]) for g in heads]
    probs = [jnp.exp2(scores[g] - tops[g]) for g in heads]
    dens = [jnp.sum(probs[g], axis=-1, keepdims=True) + jnp.exp2(sinks[g] - tops[g]) for g in heads]
    outs = [_dot(probs[g].astype(BF16), vcat) for g in heads]
    for g in heads:
        o_ref[:, cols[g]] = (outs[g] / dens[g]).astype(o_ref.dtype)


def _attn_b(qk, v, qk_c, v_c, sink, tq_pref=256):
    bsz, t, _ = qk.shape
    n_ctx = qk_c.shape[1]
    tq = _pick(t, tq_pref, WINDOW)
    r = tq // WINDOW
    n_w = t // WINDOW
    gw = GROUP * HEAD_DIM
    kb, vb = 18, 2
    prev = lambda c: (lambda b, h, i: (b, jnp.maximum(i * r - 1, 0), c + h))
    main = lambda c: (lambda b, h, i: (b, i, c + h))
    nxt = lambda c: (lambda b, h, i: (b, jnp.minimum((i + 1) * r, n_w - 1), c + h))
    nk = n_ctx + tq + 2 * WINDOW
    n_i = t // tq
    row = np.arange(tq)[:, None]
    col = np.arange(nk)[None, :]
    band = (col < n_ctx) | (np.abs(col - n_ctx - WINDOW - row) <= WINDOW)
    before = (col >= n_ctx) & (col < n_ctx + WINDOW)
    after = col >= n_ctx + WINDOW + tq
    bias = np.stack([np.where(band & ~(before & bool(v & 1)) & ~(after & bool(v & 2)), 0.0, NEG_INF)
                     for v in range(4)]).astype(np.float32)
    variant = lambda i: jnp.where(i == 0, 1, 0) + jnp.where(i == n_i - 1, 2, 0)
    return pl.pallas_call(
        _attn_b_kernel,
        grid=(bsz, N_KV_HEADS, n_i),
        in_specs=[pl.BlockSpec(memory_space=pltpu.SMEM),
                  pl.BlockSpec((None, tq, nk), lambda b, h, i: (variant(i), 0, 0)),
                  pl.BlockSpec((None, tq, gw), lambda b, h, i: (b, i, 2 + h)),
                  pl.BlockSpec((None, n_ctx, HEAD_DIM), lambda b, h, i: (b, 0, kb + h)),
                  pl.BlockSpec((None, n_ctx, HEAD_DIM), lambda b, h, i: (b, 0, vb + h)),
                  pl.BlockSpec((None, WINDOW, HEAD_DIM), prev(kb)),
                  pl.BlockSpec((None, tq, HEAD_DIM), main(kb)),
                  pl.BlockSpec((None, WINDOW, HEAD_DIM), nxt(kb)),
                  pl.BlockSpec((None, WINDOW, HEAD_DIM), prev(vb)),
                  pl.BlockSpec((None, tq, HEAD_DIM), main(vb)),
                  pl.BlockSpec((None, WINDOW, HEAD_DIM), nxt(vb))],
        out_specs=pl.BlockSpec((None, tq, gw), lambda b, h, i: (b, i, h)),
        out_shape=jax.ShapeDtypeStruct((bsz, t, N_KV_HEADS * gw), BF16),
        scratch_shapes=[pltpu.VMEM((nk, HEAD_DIM), BF16), pltpu.VMEM((nk, HEAD_DIM), BF16)],
        compiler_params=_params(("arbitrary", "arbitrary", "arbitrary")),
        name="attn_window",
    )(sink, jnp.asarray(bias), qk, qk_c, v_c, qk, qk, qk, v, v, v)


def _attn_ctx_kernel(sink_ref, q_ref, k_ref, v_ref, o_ref):
    grp = pl.program_id(1)
    k = k_ref[...]
    v = v_ref[...]
    for g in range(GROUP):
        cols = slice(g * HEAD_DIM, (g + 1) * HEAD_DIM)
        s = _dot_nt(q_ref[:, cols], k)
        o_ref[:, cols] = _softmax_out(s, sink_ref[grp * GROUP + g], v).astype(o_ref.dtype)


def _attn_ctx(qk_c, v_c, sink_all):
    bsz, n_ctx, _ = qk_c.shape
    gw = GROUP * HEAD_DIM
    n_grp = 2 * N_KV_HEADS
    return pl.pallas_call(
        _attn_ctx_kernel,
        grid=(bsz, n_grp),
        in_specs=[pl.BlockSpec(memory_space=pltpu.SMEM),
                  pl.BlockSpec((None, n_ctx, gw), lambda b, h: (b, 0, h)),
                  pl.BlockSpec((None, n_ctx, HEAD_DIM), lambda b, h: (b, 0, 16 + h)),
                  pl.BlockSpec((None, n_ctx, HEAD_DIM), lambda b, h: (b, 0, h))],
        out_specs=pl.BlockSpec((None, n_ctx, gw), lambda b, h: (b, 0, h)),
        out_shape=jax.ShapeDtypeStruct((bsz, n_ctx, n_grp * gw), BF16),
        compiler_params=_params(("arbitrary", "arbitrary")),
        name="attn_ctx",
    )(sink_all, qk_c, qk_c, v_c)


def _scan_consts(chunk, reverse, base):
    idx = np.arange(chunk)
    t = idx[:, None]
    s = idx[None, :]
    tri = ((s >= t) if reverse else (s <= t)).astype(np.float32)
    blocks = [tri]
    masks = []
    h = chunk // 2
    while h >= base:
        same = (t // (2 * h)) == (s // (2 * h))
        t_hi = (t // h) % 2 == 1
        s_hi = (s // h) % 2 == 1
        masks.append(same & (~t_hi & s_hi if reverse else t_hi & ~s_hi))
        ref = (idx // (2 * h)) * (2 * h) + (h if reverse else h - 1)
        blocks.append(tri - tri[ref, :])
        h //= 2
    same = (t // base) == (s // base)
    masks.append(same & ((s >= t) if reverse else (s <= t)))
    if base > 1:
        blocks.append(tri * same)
    dist = np.concatenate(blocks, axis=0)
    return np.stack(masks).astype(np.float32), np.concatenate([dist, dist], axis=1)


def _block_prep(order, rows, dist, reverse, base, k_ref, lf_ref, v_ref, q_ref=None, masks=None):
    chunk = rows(order[0]).stop - rows(order[0]).start
    n_level = (chunk // base).bit_length() - 1
    dists = {}
    for pos in range(0, len(order), 2):
        pair = order[pos:pos + 2]
        cols = []
        for c in pair:
            lf = lf_ref[rows(c), :]
            hi = lf.astype(BF16)
            cols.append(jnp.concatenate([hi, (lf - hi.astype(F32)).astype(BF16)], axis=0))
        dd = _dot(dist, cols[0] if len(cols) == 1 else jnp.concatenate(cols, axis=1))
        for i, c in enumerate(pair):
            dists[c] = dd[:, i * REC_DK:(i + 1) * REC_DK]
    preps = {}
    for c in order:
        d = dists[c]
        cum = d[0:chunk]
        total = cum[0:1, :] if reverse else cum[chunk - 1:chunk, :]
        k = k_ref[rows(c), :]
        prep = dict(decay=jnp.exp2(total), k_out=k * jnp.exp2(total - cum).astype(BF16))
        if q_ref is not None:
            q = q_ref[rows(c), :]
            prep["q_in"] = q * jnp.exp2(cum).astype(BF16)
            pairs = []
            for level in range(n_level):
                e = jnp.exp2(-jnp.abs(d[(level + 1) * chunk:(level + 2) * chunk])).astype(BF16)
                pairs.append((q * e, k * e))
            if base == 1:
                pairs.append((q, k))
            else:
                since = d[(n_level + 1) * chunk:(n_level + 2) * chunk]
                pairs.append((q * jnp.exp2(since).astype(BF16), k * jnp.exp2(-since).astype(BF16)))
            prep["pairs"] = pairs
        preps[c] = prep
    for c in order:
        prep = preps[c]
        prep["kv"] = _dot(v_ref[rows(c), :].astype(F32).T.astype(BF16), prep.pop("k_out"))
        if q_ref is not None:
            prep["terms"] = [_dot_nt(qe, ke) for qe, ke in prep.pop("pairs")]
    if q_ref is not None:
        for c in order:
            terms = preps[c].pop("terms")
            scores = masks[0] * terms[0]
            for level in range(1, len(terms)):
                scores = scores + masks[level] * terms[level]
            preps[c]["scores"] = scores.astype(BF16)
        for c in order:
            preps[c]["intra"] = _dot(preps[c].pop("scores"), v_ref[rows(c), :])
    return preps


def _chunk_apply(st, prep):
    st_new = st * prep["decay"] + prep["kv"]
    if "intra" not in prep:
        return st_new, None
    return st_new, _dot_nt(prep["q_in"], st.astype(BF16)) + prep["intra"]


def _scan_order(n, reverse):
    return list(range(n - 1, -1, -1) if reverse else range(n))


def _scan_state_kernel(k_ref, lf_ref, v_ref, s0_ref, dist_ref, sfin_ref, st_ref, *, chunk, reverse):
    step = pl.program_id(2)

    @pl.when(step == 0)
    def _():
        st_ref[...] = s0_ref[...]

    rows = lambda c: slice(c * chunk, (c + 1) * chunk)
    order = _scan_order(k_ref.shape[0] // chunk, reverse)
    preps = _block_prep(order, rows, dist_ref[...], reverse, 1, k_ref, lf_ref, v_ref)
    st = st_ref[...]
    for c in order:
        st, _ = _chunk_apply(st, preps[c])
    st_ref[...] = st

    @pl.when(step == pl.num_programs(2) - 1)
    def _():
        sfin_ref[...] = st


def _scan_out_kernel(*refs, chunk, reverse, final, base):
    if final:
        q_ref, k_ref, lf_ref, v_ref, s0_ref, dist_ref, masks_ref, of_ref, g_ref, gain_ref, o_ref, st_ref = refs
    else:
        q_ref, k_ref, lf_ref, v_ref, s0_ref, dist_ref, masks_ref, o_ref, st_ref = refs

    @pl.when(pl.program_id(2) == 0)
    def _():
        st_ref[...] = s0_ref[...]

    rows = lambda c: slice(c * chunk, (c + 1) * chunk)
    order = _scan_order(k_ref.shape[0] // chunk, reverse)
    preps = _block_prep(order, rows, dist_ref[...], reverse, base, k_ref, lf_ref, v_ref, q_ref, masks_ref[...])
    st = st_ref[...]
    for c in order:
        st, out = _chunk_apply(st, preps[c])
        if final:
            tot = out + of_ref[rows(c), :]
            ms = jnp.mean(tot * tot, axis=-1, keepdims=True)
            gate = g_ref[rows(c), :].astype(F32)
            out = tot * lax.rsqrt(ms + EPS) * gain_ref[...] * (gate * jax.nn.sigmoid(gate))
        o_ref[rows(c), :] = out.astype(o_ref.dtype)
    st_ref[...] = st


def _scan_geometry(t, blk_pref=4096, chunk_pref=128):
    chunk = _pick(t, chunk_pref, LANES)
    blk = _pick(t, blk_pref, chunk)
    return chunk, blk


def _scan_state(kf, lf, qig, s0, direction):
    bsz, t, _ = kf.shape
    n_h = s0.shape[1]
    reverse = direction == 1
    chunk, blk = _scan_geometry(t)
    n_blk = t // blk
    dist = _scan_consts(chunk, reverse, 1)[1][:chunk]
    pos = (lambda c: n_blk - 1 - c) if reverse else (lambda c: c)
    kcol = direction * n_h
    return pl.pallas_call(
        functools.partial(_scan_state_kernel, chunk=chunk, reverse=reverse),
        grid=(bsz, n_h, n_blk),
        in_specs=[pl.BlockSpec((None, blk, REC_DK), lambda b, h, c: (b, pos(c), kcol + h)),
                  pl.BlockSpec((None, blk, REC_DK), lambda b, h, c: (b, pos(c), kcol + h)),
                  pl.BlockSpec((None, blk, REC_DK), lambda b, h, c: (b, pos(c), n_h + h)),
                  pl.BlockSpec((None, None, REC_DK, REC_DK), lambda b, h, c: (b, h, 0, 0)),
                  pl.BlockSpec(dist.shape, lambda b, h, c: (0, 0))],
        out_specs=pl.BlockSpec((None, None, REC_DK, REC_DK), lambda b, h, c: (b, h, 0, 0)),
        out_shape=jax.ShapeDtypeStruct(s0.shape, F32),
        scratch_shapes=[pltpu.VMEM((REC_DK, REC_DK), F32)],
        compiler_params=_params(("arbitrary", "arbitrary", "arbitrary")),
        name="hgrn_state_bwd" if reverse else "hgrn_state_fwd",
    )(kf, lf, qig, s0, jnp.asarray(dist, BF16))


def _scan_out(kf, lf, qig, s0, direction, base, other=None, out_gain=None):
    bsz, t, _ = kf.shape
    n_h = s0.shape[1]
    reverse = direction == 1
    final = other is not None
    chunk, blk = _scan_geometry(t)
    n_blk = t // blk
    masks, dist = _scan_consts(chunk, reverse, base)
    pos = (lambda c: n_blk - 1 - c) if reverse else (lambda c: c)
    kcol = direction * n_h
    tok = lambda col: pl.BlockSpec((None, blk, REC_DK), lambda b, h, c: (b, pos(c), col + h))
    in_specs = [tok(0), tok(kcol), tok(kcol), tok(n_h),
                pl.BlockSpec((None, None, REC_DK, REC_DK), lambda b, h, c: (b, h, 0, 0)),
                pl.BlockSpec(dist.shape, lambda b, h, c: (0, 0)),
                pl.BlockSpec(masks.shape, lambda b, h, c: (0, 0, 0))]
    args = [qig, kf, lf, qig, s0, jnp.asarray(dist, BF16), jnp.asarray(masks)]
    if final:
        in_specs += [tok(0), tok(2 * n_h), pl.BlockSpec((1, REC_DK), lambda b, h, c: (0, 0))]
        args += [other, qig, out_gain.reshape(1, REC_DK)]
    return pl.pallas_call(
        functools.partial(_scan_out_kernel, chunk=chunk, reverse=reverse, final=final, base=base),
        grid=(bsz, n_h, n_blk),
        in_specs=in_specs,
        out_specs=tok(0),
        out_shape=jax.ShapeDtypeStruct((bsz, t, n_h * REC_DK), BF16 if final else F32),
        scratch_shapes=[pltpu.VMEM((REC_DK, REC_DK), F32)],
        compiler_params=_params(("arbitrary", "arbitrary", "arbitrary")),
        name="hgrn_scan_bwd" if reverse else "hgrn_scan_fwd",
    )(*args)


def _rope_tables(n_tokens):
    n_rows = n_tokens // GRID_W
    row = jnp.repeat(jnp.arange(n_rows, dtype=F32), GRID_W)
    col = jnp.tile(jnp.arange(GRID_W, dtype=F32), n_rows)
    quarter = HEAD_DIM // 4
    inv_freq = ROPE_THETA ** (-jnp.arange(quarter, dtype=F32) / quarter)
    ang = jnp.concatenate([row[:, None] * inv_freq, col[:, None] * inv_freq], axis=-1)
    cos, sin = jnp.cos(ang), jnp.sin(ang)
    return jnp.concatenate([cos, cos], axis=-1), jnp.concatenate([-sin, sin], axis=-1)


def _conv_ffn(x, gain, shift, scale, gate, w_up, conv_w, conv_b, w_down):
    act = _ffn_up(x, gain, shift, scale, w_up, conv_w, conv_b)
    return _mm_res([act], [w_down], x, gate)


def kernel(x, c, ctx, c_ctx, w_mod, b_mod, norm_gain, attn_w_in, attn_w_out, attn_qk_gain, attn_sink,
           rec_w_in, rec_w_out, rec_out_gain, rec_lb_gamma, ffn_w_up, ffn_conv_w, ffn_conv_b, ffn_w_down):
    bsz, n_lat, d = x.shape
    depth = w_mod.shape[0]
    assert depth == 2 and bsz < SUBLANES, "layer 0 attention, layer 1 recurrence; conditioning rows fit one tile"
    n_heads_rec = d // REC_DK

    cvec = jnp.zeros((SUBLANES, d), F32).at[:bsz].set(c).at[bsz].set(c_ctx)
    mods = _mods(cvec, w_mod, b_mod)

    def lat_mod(layer, k):
        return mods[layer, :bsz, k * d:(k + 1) * d][:, None, :]

    def ctx_mod(layer, k):
        return jnp.broadcast_to(mods[layer, bsz, k * d:(k + 1) * d][None, None, :], (bsz, 1, d))

    w_up = ffn_w_up.astype(BF16)
    w_down = ffn_w_down.astype(BF16)

    aq = N_Q_HEADS * HEAD_DIM
    akv = N_KV_HEADS * HEAD_DIM
    w_in = attn_w_in[0]
    qa, ka, va, qb, kb, vb = jnp.split(w_in, np.cumsum([aq, akv, akv, aq, akv])[:5].tolist(), axis=1)
    w_qk = jnp.concatenate([qa, qb, ka, kb], axis=1).astype(BF16)
    w_v = jnp.concatenate([va, vb], axis=1).astype(BF16)
    qkg = attn_qk_gain[0]
    q_scale = ATTN_SCALE * LOG2E
    head_gain = jnp.concatenate([jnp.tile(qkg[0] * q_scale, N_Q_HEADS), jnp.tile(qkg[2] * q_scale, N_Q_HEADS),
                                 jnp.tile(qkg[1], N_KV_HEADS), jnp.tile(qkg[3], N_KV_HEADS)])[None, :]
    cos, sin = _rope_tables(n_lat)
    g0 = norm_gain[0, 0]
    qk_l = _proj_qk(x, g0, lat_mod(0, 0), lat_mod(0, 1), w_qk, head_gain, cos, sin)
    v_l = _proj_plain(x, g0, lat_mod(0, 0), lat_mod(0, 1), w_v)
    qk_c = _proj_qk(ctx, g0, ctx_mod(0, 0), ctx_mod(0, 1), w_qk, head_gain)
    v_c = _proj_plain(ctx, g0, ctx_mod(0, 0), ctx_mod(0, 1), w_v)

    ka_cols = slice(2 * aq, 2 * aq + akv)
    k_all = jnp.concatenate([qk_c[:, :, ka_cols], qk_l[:, :, ka_cols]], axis=1)
    v_all = jnp.concatenate([v_c[:, :, :akv], v_l[:, :, :akv]], axis=1)
    ones = jnp.ones(v_all.shape[:2] + (HEAD_DIM,), BF16)
    v_ext = jnp.concatenate([v_all[:, :, :HEAD_DIM], ones, v_all[:, :, HEAD_DIM:], ones], axis=-1)
    sink = attn_sink[0].astype(F32)
    o_a = _attn_a(qk_l, k_all, v_ext)
    o_b = _attn_b(qk_l, v_l, qk_c, v_c, sink)
    o_c = _attn_ctx(qk_c, v_c, jnp.concatenate([jnp.full((N_Q_HEADS,), -jnp.inf, F32), sink]))

    w_out = attn_w_out[0].astype(BF16)
    x = _mm_res([o_a, o_b], [w_out[:aq], w_out[aq:]], x, lat_mod(0, 2))
    ctx = _mm_res([o_c], [w_out], ctx, ctx_mod(0, 2))
    g1 = norm_gain[0, 1]
    x = _conv_ffn(x, g1, lat_mod(0, 3), lat_mod(0, 4), lat_mod(0, 5), w_up[0], ffn_conv_w[0], ffn_conv_b[0], w_down[0])
    ctx = _conv_ffn(ctx, g1, ctx_mod(0, 3), ctx_mod(0, 4), ctx_mod(0, 5), w_up[0], ffn_conv_w[0], ffn_conv_b[0],
                    w_down[0])

    lb_soft = jax.nn.softmax(rec_lb_gamma.astype(F32), axis=1)
    lower = (jnp.cumsum(lb_soft, axis=1) - lb_soft[:, :1])[:, 1]
    rk = n_heads_rec * REC_DK
    w_q, w_ff, w_fb, w_i, w_g = jnp.split(rec_w_in[0], [rk, 2 * rk, 3 * rk, 3 * rk + d], axis=1)
    w_qig = jnp.concatenate([w_q, w_i, w_g], axis=1).astype(BF16)
    w_f = jnp.concatenate([w_ff, w_fb], axis=1).astype(BF16)
    lb = lower.reshape(1, 2 * rk)
    g0 = norm_gain[1, 0]
    qig_l = _proj_plain(x, g0, lat_mod(1, 0), lat_mod(1, 1), w_qig)
    lf_l, kf_l = _proj_gate(x, g0, lat_mod(1, 0), lat_mod(1, 1), w_f, lb)
    qig_c = _proj_plain(ctx, g0, ctx_mod(1, 0), ctx_mod(1, 1), w_qig)
    lf_c, kf_c = _proj_gate(ctx, g0, ctx_mod(1, 0), ctx_mod(1, 1), w_f, lb)

    s_zero = jnp.zeros((bsz, n_heads_rec, REC_DK, REC_DK), F32)
    s_fwd = _scan_state(kf_c, lf_c, qig_c, s_zero, 0)
    s_bwd = _scan_state(kf_c, lf_c, qig_c, s_zero, 1)

    def scans(base):
        def run(kf, lf, qig, s_f, s_b, gain):
            o_f = _scan_out(kf, lf, qig, s_f, 0, base)
            return _scan_out(kf, lf, qig, s_b, 1, base, other=o_f, out_gain=gain)
        return run

    worst_exp2 = FAST_BASE * jnp.max(-jnp.log2(lower))
    y = lax.cond(worst_exp2 <= MAX_EXP2, scans(FAST_BASE), scans(1),
                 kf_l, lf_l, qig_l, s_fwd, s_bwd, rec_out_gain[0])
    x = _mm_res([y], [rec_w_out[0].astype(BF16)], x, lat_mod(1, 2))
    g1 = norm_gain[1, 1]
    x = _conv_ffn(x, g1, lat_mod(1, 3), lat_mod(1, 4), lat_mod(1, 5), w_up[1], ffn_conv_w[1], ffn_conv_b[1], w_down[1])
    return x
```

```python
import functools
import math

import numpy as np
import jax
import jax.numpy as jnp
from jax import lax
from jax.experimental import pallas as pl
from jax.experimental.pallas import tpu as pltpu

F32 = jnp.float32
BF16 = jnp.bfloat16

HEAD_DIM = 128
GRID_W = 64
N_Q_HEADS = 8
N_KV_HEADS = 2
GROUP = N_Q_HEADS // N_KV_HEADS
WINDOW = 128
ROPE_THETA = 10000.0
ATTN_SCALE = HEAD_DIM ** -0.5
NEG_INF = -1e30
LOG2E = 1.4426950408889634
REC_DK = 128
FAST_BASE = 64
MAX_EXP2 = 100.0
N_MOD = 6
EPS = 1e-6

LANES = 128
SUBLANES = 8
BF16_ROWS = 16
MXU_COLS = 256
CHAINS_PER_STAGE = 4
VMEM_LIMIT = 56 * 1024 * 1024
W_TILE_BYTES = 6 * 1024 * 1024


def _pick(n, pref, align):
    if n <= pref:
        return n
    t = (pref // align) * align
    while t >= align:
        if n % t == 0:
            return t
        t -= align
    raise ValueError(f"no tile for {n} (pref {pref}, align {align})")


def _params(sem):
    return pltpu.CompilerParams(dimension_semantics=sem, vmem_limit_bytes=VMEM_LIMIT)


def _dot(a, b):
    return jnp.dot(a, b, preferred_element_type=F32)


def _dot_nt(a, b):
    return lax.dot_general(a, b, (((1,), (1,)), ((), ())), preferred_element_type=F32)


def _mod_kernel(c_ref, w_ref, b_ref, o_ref):
    cv = c_ref[...]
    o_ref[...] = _dot(cv * jax.nn.sigmoid(cv), w_ref[...]) + b_ref[...]


def _mods(cvec, w_mod, b_mod):
    depth, d, n = w_mod.shape
    tn = _pick(n, 1024, LANES)
    return pl.pallas_call(
        _mod_kernel,
        grid=(depth, n // tn),
        in_specs=[pl.BlockSpec((SUBLANES, d), lambda l, j: (0, 0)),
                  pl.BlockSpec((None, d, tn), lambda l, j: (l, 0, j)),
                  pl.BlockSpec((None, 1, tn), lambda l, j: (l, 0, j))],
        out_specs=pl.BlockSpec((None, SUBLANES, tn), lambda l, j: (l, 0, j)),
        out_shape=jax.ShapeDtypeStruct((depth, SUBLANES, n), F32),
        compiler_params=_params(("arbitrary", "arbitrary")),
        name="adaln_mods",
    )(cvec, w_mod, b_mod.reshape(depth, 1, n))


def _norm_mod(x, g, sh, sc):
    ms = jnp.mean(x * x, axis=-1, keepdims=True)
    return (x * lax.rsqrt(ms + EPS)) * (g * (1.0 + sc)) + sh


def _fill_h(x_ref, g_ref, sh_ref, sc_ref, h_ref):
    @pl.when(pl.program_id(2) == 0)
    def _():
        h_ref[...] = _norm_mod(x_ref[...], g_ref[...], sh_ref[...], sc_ref[...]).astype(BF16)


def _plain_blocks(h, w_ref, start, width):
    return [_dot(h, w_ref[:, start + c0:start + min(c0 + MXU_COLS, width)]) for c0 in range(0, width, MXU_COLS)]


def _store_blocks(o_ref, blocks):
    c0 = 0
    for blk in blocks:
        o_ref[:, c0:c0 + blk.shape[1]] = blk.astype(o_ref.dtype)
        c0 += blk.shape[1]


def _proj_qkv_kernel(*refs, rope):
    if rope:
        x_ref, g_ref, sh_ref, sc_ref, w_ref, hg_ref, cos_ref, sin_ref, qk_ref, v_ref, h_ref = refs
    else:
        x_ref, g_ref, sh_ref, sc_ref, w_ref, hg_ref, qk_ref, v_ref, h_ref = refs
    _fill_h(x_ref, g_ref, sh_ref, sc_ref, h_ref)
    h = h_ref[...]
    n_qk = qk_ref.shape[1]
    accs = _plain_blocks(h, w_ref, 0, n_qk)
    v_blocks = _plain_blocks(h, w_ref, n_qk, v_ref.shape[1])
    heads = [acc[:, c:c + HEAD_DIM] for acc in accs for c in range(0, acc.shape[1], HEAD_DIM)]
    cols = [slice(c, c + HEAD_DIM) for c in range(0, n_qk, HEAD_DIM)]
    inv = [lax.rsqrt(jnp.mean(x * x, axis=-1, keepdims=True) + EPS) for x in heads]
    ys = [x * r * hg_ref[:, c] for x, r, c in zip(heads, inv, cols)]
    if rope:
        ys = [y * cos_ref[...] + pltpu.roll(y, HEAD_DIM // 2, 1) * sin_ref[...] for y in ys]
    for y, c in zip(ys, cols):
        qk_ref[:, c] = y.astype(qk_ref.dtype)
    _store_blocks(v_ref, v_blocks)


def _proj_rec_kernel(x_ref, g_ref, sh_ref, sc_ref, w_ref, lb_ref, lf_ref, k_ref, qig_ref, h_ref):
    _fill_h(x_ref, g_ref, sh_ref, sc_ref, h_ref)
    h = h_ref[...]
    n_f = lf_ref.shape[1]
    sigs = [jax.nn.sigmoid(z) for z in _plain_blocks(h, w_ref, 0, n_f)]
    plain = _plain_blocks(h, w_ref, n_f, qig_ref.shape[1])
    c0 = 0
    for sig in sigs:
        c = slice(c0, c0 + sig.shape[1])
        lb = lb_ref[:, c]
        lf_ref[:, c] = jnp.log2(lb + (1.0 - lb) * sig)
        k_ref[:, c] = ((1.0 - lb) * (1.0 - sig)).astype(k_ref.dtype)
        c0 += sig.shape[1]
    _store_blocks(qig_ref, plain)


def _interleave(w_a, w_b, n_step):
    d = w_a.shape[0]
    return jnp.concatenate([w_a.reshape(d, n_step, -1), w_b.reshape(d, n_step, -1)], axis=2).reshape(d, -1)


def _proj_call(kernel, name, x, gain, shift, scale, w, n_step, extras, extra_specs, outs, tm_pref=1024):
    bsz, t, d = x.shape
    tm = _pick(t, tm_pref, BF16_ROWS)
    in_specs = [pl.BlockSpec((None, tm, d), lambda b, i, j: (b, i, 0)),
                pl.BlockSpec((1, d), lambda b, i, j: (0, 0)),
                pl.BlockSpec((None, 1, d), lambda b, i, j: (b, 0, 0)),
                pl.BlockSpec((None, 1, d), lambda b, i, j: (b, 0, 0)),
                pl.BlockSpec((d, w.shape[1] // n_step), lambda b, i, j: (0, j))] + [s(tm) for s in extra_specs]
    return pl.pallas_call(
        kernel,
        grid=(bsz, t // tm, n_step),
        in_specs=in_specs,
        out_specs=[pl.BlockSpec((None, tm, n // n_step), lambda b, i, j: (b, i, j)) for n, _ in outs],
        out_shape=[jax.ShapeDtypeStruct((bsz, t, n), dt) for n, dt in outs],
        scratch_shapes=[pltpu.VMEM((tm, d), BF16)],
        compiler_params=_params(("arbitrary", "arbitrary", "arbitrary")),
        name=name,
    )(x, gain.reshape(1, d), shift, scale, w, *extras)


def _col_spec(width):
    return lambda tm: pl.BlockSpec((1, width), lambda b, i, j: (0, j))


def _pos_spec(tm):
    return pl.BlockSpec((tm, HEAD_DIM), lambda b, i, j: (i, 0))


def _n_steps(n_a, n_b):
    return math.gcd(n_a // MXU_COLS, n_b // MXU_COLS)


def _proj_qkv(x, gain, shift, scale, w_qk, w_v, head_gain, cos=None, sin=None):
    n_qk, n_v = w_qk.shape[1], w_v.shape[1]
    n_step = _n_steps(n_qk, n_v)
    w = _interleave(w_qk, w_v, n_step)
    outs = ((n_qk, BF16), (n_v, BF16))
    hg_spec = _col_spec(n_qk // n_step)
    if cos is None:
        return _proj_call(functools.partial(_proj_qkv_kernel, rope=False), "proj_qkv_ctx", x, gain, shift, scale, w,
                          n_step, (head_gain,), (hg_spec,), outs)
    return _proj_call(functools.partial(_proj_qkv_kernel, rope=True), "proj_qkv_rope", x, gain, shift, scale, w,
                      n_step, (head_gain, cos, sin), (hg_spec, _pos_spec, _pos_spec), outs)


def _proj_rec(x, gain, shift, scale, w_f, w_qig, lb):
    n_f, n_p = w_f.shape[1], w_qig.shape[1]
    n_step = _n_steps(n_f, n_p)
    outs = ((n_f, F32), (n_f, BF16), (n_p, BF16))
    return _proj_call(_proj_rec_kernel, "proj_rec", x, gain, shift, scale, _interleave(w_f, w_qig, n_step), n_step,
                      (lb,), (_col_spec(n_f // n_step),), outs)


def _mm_res_kernel(a_ref, w_ref, r_ref, gt_ref, o_ref):
    o_ref[...] = r_ref[...] + gt_ref[...] * _dot(a_ref[...], w_ref[...])


def _mm2_res_kernel(a1_ref, a2_ref, w1_ref, w2_ref, r_ref, gt_ref, o_ref):
    acc = _dot(a1_ref[...], w1_ref[...]) + _dot(a2_ref[...], w2_ref[...])
    o_ref[...] = r_ref[...] + gt_ref[...] * acc


def _mm_res(acts, ws, res, gate, tm_pref=1024):
    bsz, t, n = res.shape
    tm = _pick(t, tm_pref, BF16_ROWS)
    k_total = sum(w.shape[0] for w in ws)
    tn = _pick(n, max(LANES, W_TILE_BYTES // (2 * k_total)), LANES)
    kernel = _mm_res_kernel if len(acts) == 1 else _mm2_res_kernel
    in_specs = ([pl.BlockSpec((None, tm, a.shape[2]), lambda b, i, j: (b, i, 0)) for a in acts]
                + [pl.BlockSpec((w.shape[0], tn), lambda b, i, j: (0, j)) for w in ws]
                + [pl.BlockSpec((None, tm, tn), lambda b, i, j: (b, i, j)),
                   pl.BlockSpec((None, 1, tn), lambda b, i, j: (b, 0, j))])
    return pl.pallas_call(
        kernel,
        grid=(bsz, t // tm, n // tn),
        in_specs=in_specs,
        out_specs=pl.BlockSpec((None, tm, tn), lambda b, i, j: (b, i, j)),
        out_shape=jax.ShapeDtypeStruct((bsz, t, n), F32),
        compiler_params=_params(("arbitrary", "arbitrary", "arbitrary")),
        name="mm_res" if len(acts) == 1 else "mm2_res",
    )(*acts, *ws, res, gate)


def _ffn_up_kernel(xp_ref, x_ref, xn_ref, g_ref, sh_ref, sc_ref, wg_ref, wv_ref, cwg_ref, cwv_ref,
                   cbg_ref, cbv_ref, o_ref, h_ref, *, tm, halo):
    i = pl.program_id(1)
    last = pl.num_programs(1) - 1

    @pl.when(pl.program_id(2) == 0)
    def _():
        g, sh, sc = g_ref[...], sh_ref[...], sc_ref[...]
        hp = jnp.where(i > 0, _norm_mod(xp_ref[...], g, sh, sc), 0.0)
        hn = jnp.where(i < last, _norm_mod(xn_ref[...], g, sh, sc), 0.0)
        h_ref[0:halo, :] = hp.astype(BF16)
        h_ref[halo:halo + tm, :] = _norm_mod(x_ref[...], g, sh, sc).astype(BF16)
        h_ref[halo + tm:, :] = hn.astype(BF16)

    h = h_ref[...]
    rows = tm + 2 * halo

    def branch(w_ref, cw_ref, cb_ref):
        u = _dot(h, w_ref[...])
        u_prev = pltpu.roll(u, 1, 0)[halo:halo + tm]
        u_next = pltpu.roll(u, rows - 1, 0)[halo:halo + tm]
        cw = cw_ref[...]
        return u_prev * cw[0:1] + u[halo:halo + tm] * cw[1:2] + u_next * cw[2:3] + cb_ref[...]

    yg = branch(wg_ref, cwg_ref, cbg_ref)
    yv = branch(wv_ref, cwv_ref, cbv_ref)
    o_ref[...] = (yg * jax.nn.sigmoid(yg) * yv).astype(o_ref.dtype)


def _ffn_up(x, gain, shift, scale, w_up, conv_w, conv_b, tm_pref=1024, tn_pref=512):
    bsz, t, d = x.shape
    d_ff = w_up.shape[1] // 2
    halo = BF16_ROWS
    tm = _pick(t, tm_pref, halo)
    tn = _pick(d_ff, tn_pref, LANES)
    nj = d_ff // tn
    r = tm // halo
    n_halo = t // halo
    kern = functools.partial(_ffn_up_kernel, tm=tm, halo=halo)
    conv_b = conv_b.reshape(1, 2 * d_ff)
    vec = lambda b, i, j: (b, 0, 0)
    return pl.pallas_call(
        kern,
        grid=(bsz, t // tm, nj),
        in_specs=[pl.BlockSpec((None, halo, d), lambda b, i, j: (b, jnp.maximum(i * r - 1, 0), 0)),
                  pl.BlockSpec((None, tm, d), lambda b, i, j: (b, i, 0)),
                  pl.BlockSpec((None, halo, d), lambda b, i, j: (b, jnp.minimum((i + 1) * r, n_halo - 1), 0)),
                  pl.BlockSpec((1, d), lambda b, i, j: (0, 0)),
                  pl.BlockSpec((None, 1, d), vec),
                  pl.BlockSpec((None, 1, d), vec),
                  pl.BlockSpec((d, tn), lambda b, i, j: (0, j)),
                  pl.BlockSpec((d, tn), lambda b, i, j: (0, j + nj)),
                  pl.BlockSpec((conv_w.shape[0], tn), lambda b, i, j: (0, j)),
                  pl.BlockSpec((conv_w.shape[0], tn), lambda b, i, j: (0, j + nj)),
                  pl.BlockSpec((1, tn), lambda b, i, j: (0, j)),
                  pl.BlockSpec((1, tn), lambda b, i, j: (0, j + nj))],
        out_specs=pl.BlockSpec((None, tm, tn), lambda b, i, j: (b, i, j)),
        out_shape=jax.ShapeDtypeStruct((bsz, t, d_ff), BF16),
        scratch_shapes=[pltpu.VMEM((tm + 2 * halo, d), BF16)],
        compiler_params=_params(("arbitrary", "arbitrary", "arbitrary")),
        name="ffn_up_conv",
    )(x, x, x, gain.reshape(1, d), shift, scale, w_up, w_up, conv_w, conv_w, conv_b, conv_b)


def _attn_a_kernel(q_ref, k_ref, v_ref, o_ref, m_ref, acc_ref, *, tq, rs):
    step = pl.program_id(3)

    @pl.when(step == 0)
    def _():
        m_ref[...] = jnp.full(m_ref.shape, -jnp.inf, F32)
        acc_ref[...] = jnp.zeros(acc_ref.shape, F32)

    k = k_ref[...]
    v = v_ref[...]
    n_chunk = k.shape[0] // LANES
    chains = [(g, r) for g in range(GROUP) for r in range(tq // rs)]
    for start in range(0, len(chains), CHAINS_PER_STAGE):
        group = chains[start:start + CHAINS_PER_STAGE]
        rows = [slice(g * tq + r * rs, g * tq + (r + 1) * rs) for g, r in group]
        scores = [_dot_nt(q_ref[r * rs:(r + 1) * rs, g * HEAD_DIM:(g + 1) * HEAD_DIM], k) for g, r in group]
        chunks = [[s[:, c * LANES:(c + 1) * LANES] for c in range(n_chunk)] for s in scores]
        m_prev = [m_ref[rw, :] for rw in rows]
        m_new = [jnp.maximum(mp, jnp.max(functools.reduce(jnp.maximum, ch), axis=-1, keepdims=True))
                 for mp, ch in zip(m_prev, chunks)]
        probs = [jnp.concatenate([jnp.exp2(c - mn) for c in ch], axis=1).astype(BF16) for mn, ch in zip(m_new, chunks)]
        pv = [_dot(p, v) for p in probs]
        for rw, mp, mn, upd in zip(rows, m_prev, m_new, pv):
            alpha = jnp.exp2(mp - mn)
            acc_ref[rw, :] = jnp.concatenate([alpha, alpha], axis=1) * acc_ref[rw, :] + upd
            m_ref[rw, :] = mn

    @pl.when(step == pl.num_programs(3) - 1)
    def _():
        for g in range(GROUP):
            acc = acc_ref[g * tq:(g + 1) * tq, :]
            o_ref[:, g * HEAD_DIM:(g + 1) * HEAD_DIM] = (acc[:, :HEAD_DIM] / acc[:, HEAD_DIM:]).astype(o_ref.dtype)


def _attn_a(qk, k_all, v_ext, tq_pref=1024, tk_pref=3328, rs_pref=256):
    bsz, t, _ = qk.shape
    s = k_all.shape[1]
    tq = _pick(t, tq_pref, BF16_ROWS)
    rs = _pick(tq, rs_pref, BF16_ROWS)
    tk = _pick(s, tk_pref, LANES)
    gw = GROUP * HEAD_DIM
    m = GROUP * tq
    return pl.pallas_call(
        functools.partial(_attn_a_kernel, tq=tq, rs=rs),
        grid=(bsz, N_KV_HEADS, t // tq, s // tk),
        in_specs=[pl.BlockSpec((None, tq, gw), lambda b, h, i, j: (b, i, h)),
                  pl.BlockSpec((None, tk, HEAD_DIM), lambda b, h, i, j: (b, j, h)),
                  pl.BlockSpec((None, tk, 2 * HEAD_DIM), lambda b, h, i, j: (b, j, h))],
        out_specs=pl.BlockSpec((None, tq, gw), lambda b, h, i, j: (b, i, h)),
        out_shape=jax.ShapeDtypeStruct((bsz, t, N_KV_HEADS * gw), BF16),
        scratch_shapes=[pltpu.VMEM((m, HEAD_DIM), F32), pltpu.VMEM((m, 2 * HEAD_DIM), F32)],
        compiler_params=_params(("arbitrary", "arbitrary", "arbitrary", "arbitrary")),
        name="attn_global",
    )(qk, k_all, v_ext)


def _softmax_out(s, sink, v):
    sink = sink * LOG2E
    m = jnp.maximum(jnp.max(s, axis=-1, keepdims=True), sink)
    p = jnp.exp2(s - m)
    den = jnp.sum(p, axis=-1, keepdims=True) + jnp.exp2(sink - m)
    return _dot(p.astype(BF16), v) / den


def _attn_b_kernel(sink_ref, bias_ref, q_ref, kc_ref, vc_ref, kp_ref, km_ref, kn_ref, vp_ref, vm_ref, vn_ref,
                   o_ref, kcat_ref, vcat_ref):
    kv = pl.program_id(1)
    for dst, parts in ((kcat_ref, (kc_ref, kp_ref, km_ref, kn_ref)), (vcat_ref, (vc_ref, vp_ref, vm_ref, vn_ref))):
        off = 0
        for part in parts:
            dst[off:off + part.shape[0], :] = part[...]
            off += part.shape[0]
    kcat = kcat_ref[...]
    vcat = vcat_ref[...]
    bias = bias_ref[...]
    heads = range(GROUP)
    cols = [slice(g * HEAD_DIM, (g + 1) * HEAD_DIM) for g in heads]
    scores = [_dot_nt(q_ref[:, cols[g]], kcat) + bias for g in heads]
    sinks = [sink_ref[kv * GROUP + g] * LOG2E for g in heads]
    tops = [jnp.maximum(jnp.max(scores[g], axis=-1, keepdims=True), sinks[g]) for g in heads]
    probs = [jnp.exp2(scores[g] - tops[g]) for g in heads]
    dens = [jnp.sum(probs[g], axis=-1, keepdims=True) + jnp.exp2(sinks[g] - tops[g]) for g in heads]
    outs = [_dot(probs[g].astype(BF16), vcat) for g in heads]
    for g in heads:
        o_ref[:, cols[g]] = (outs[g] / dens[g]).astype(o_ref.dtype)


def _attn_b(qk, v, qk_c, v_c, sink, tq_pref=256):
    bsz, t, _ = qk.shape
    n_ctx = qk_c.shape[1]
    tq = _pick(t, tq_pref, WINDOW)
    r = tq // WINDOW
    n_w = t // WINDOW
    gw = GROUP * HEAD_DIM
    kb, vb = 18, 2
    prev = lambda c: (lambda b, h, i: (b, jnp.maximum(i * r - 1, 0), c + h))
    main = lambda c: (lambda b, h, i: (b, i, c + h))
    nxt = lambda c: (lambda b, h, i: (b, jnp.minimum((i + 1) * r, n_w - 1), c + h))
    nk = n_ctx + tq + 2 * WINDOW
    n_i = t // tq
    row = np.arange(tq)[:, None]
    col = np.arange(nk)[None, :]
    band = (col < n_ctx) | (np.abs(col - n_ctx - WINDOW - row) <= WINDOW)
    before = (col >= n_ctx) & (col < n_ctx + WINDOW)
    after = col >= n_ctx + WINDOW + tq
    bias = np.stack([np.where(band & ~(before & bool(v & 1)) & ~(after & bool(v & 2)), 0.0, NEG_INF)
                     for v in range(4)]).astype(np.float32)
    variant = lambda i: jnp.where(i == 0, 1, 0) + jnp.where(i == n_i - 1, 2, 0)
    return pl.pallas_call(
        _attn_b_kernel,
        grid=(bsz, N_KV_HEADS, n_i),
        in_specs=[pl.BlockSpec(memory_space=pltpu.SMEM),
                  pl.BlockSpec((None, tq, nk), lambda b, h, i: (variant(i), 0, 0)),
                  pl.BlockSpec((None, tq, gw), lambda b, h, i: (b, i, 2 + h)),
                  pl.BlockSpec((None, n_ctx, HEAD_DIM), lambda b, h, i: (b, 0, kb + h)),
                  pl.BlockSpec((None, n_ctx, HEAD_DIM), lambda b, h, i: (b, 0, vb + h)),
                  pl.BlockSpec((None, WINDOW, HEAD_DIM), prev(kb)),
                  pl.BlockSpec((None, tq, HEAD_DIM), main(kb)),
                  pl.BlockSpec((None, WINDOW, HEAD_DIM), nxt(kb)),
                  pl.BlockSpec((None, WINDOW, HEAD_DIM), prev(vb)),
                  pl.BlockSpec((None, tq, HEAD_DIM), main(vb)),
                  pl.BlockSpec((None, WINDOW, HEAD_DIM), nxt(vb))],
        out_specs=pl.BlockSpec((None, tq, gw), lambda b, h, i: (b, i, h)),
        out_shape=jax.ShapeDtypeStruct((bsz, t, N_KV_HEADS * gw), BF16),
        scratch_shapes=[pltpu.VMEM((nk, HEAD_DIM), BF16), pltpu.VMEM((nk, HEAD_DIM), BF16)],
        compiler_params=_params(("arbitrary", "arbitrary", "arbitrary")),
        name="attn_window",
    )(sink, jnp.asarray(bias), qk, qk_c, v_c, qk, qk, qk, v, v, v)


def _attn_ctx_kernel(sink_ref, q_ref, k_ref, v_ref, o_ref):
    grp = pl.program_id(1)
    k = k_ref[...]
    v = v_ref[...]
    for g in range(GROUP):
        cols = slice(g * HEAD_DIM, (g + 1) * HEAD_DIM)
        s = _dot_nt(q_ref[:, cols], k)
        o_ref[:, cols] = _softmax_out(s, sink_ref[grp * GROUP + g], v).astype(o_ref.dtype)


def _attn_ctx(qk_c, v_c, sink_all):
    bsz, n_ctx, _ = qk_c.shape
    gw = GROUP * HEAD_DIM
    n_grp = 2 * N_KV_HEADS
    return pl.pallas_call(
        _attn_ctx_kernel,
        grid=(bsz, n_grp),
        in_specs=[pl.BlockSpec(memory_space=pltpu.SMEM),
                  pl.BlockSpec((None, n_ctx, gw), lambda b, h: (b, 0, h)),
                  pl.BlockSpec((None, n_ctx, HEAD_DIM), lambda b, h: (b, 0, 16 + h)),
                  pl.BlockSpec((None, n_ctx, HEAD_DIM), lambda b, h: (b, 0, h))],
        out_specs=pl.BlockSpec((None, n_ctx, gw), lambda b, h: (b, 0, h)),
        out_shape=jax.ShapeDtypeStruct((bsz, n_ctx, n_grp * gw), BF16),
        compiler_params=_params(("arbitrary", "arbitrary")),
        name="attn_ctx",
    )(sink_all, qk_c, qk_c, v_c)


def _scan_consts(chunk, reverse, base):
    idx = np.arange(chunk)
    t = idx[:, None]
    s = idx[None, :]
    tri = ((s >= t) if reverse else (s <= t)).astype(np.float32)
    blocks = [tri]
    masks = []
    h = chunk // 2
    while h >= base:
        same = (t // (2 * h)) == (s // (2 * h))
        t_hi = (t // h) % 2 == 1
        s_hi = (s // h) % 2 == 1
        masks.append(same & (~t_hi & s_hi if reverse else t_hi & ~s_hi))
        ref = (idx // (2 * h)) * (2 * h) + (h if reverse else h - 1)
        blocks.append(tri - tri[ref, :])
        h //= 2
    same = (t // base) == (s // base)
    masks.append(same & ((s >= t) if reverse else (s <= t)))
    if base > 1:
        blocks.append(tri * same)
    dist = np.concatenate(blocks, axis=0)
    return np.stack(masks).astype(np.float32), np.concatenate([dist, dist], axis=1)


def _block_prep(order, rows, dist, reverse, base, k_ref, lf_ref, v_ref, q_ref=None, masks=None):
    chunk = rows(order[0]).stop - rows(order[0]).start
    n_level = (chunk // base).bit_length() - 1
    dists = {}
    for pos in range(0, len(order), 2):
        pair = order[pos:pos + 2]
        cols = []
        for c in pair:
            lf = lf_ref[rows(c), :]
            hi = lf.astype(BF16)
            cols.append(jnp.concatenate([hi, (lf - hi.astype(F32)).astype(BF16)], axis=0))
        dd = _dot(dist, cols[0] if len(cols) == 1 else jnp.concatenate(cols, axis=1))
        for i, c in enumerate(pair):
            dists[c] = dd[:, i * REC_DK:(i + 1) * REC_DK]
    preps = {}
    for c in order:
        d = dists[c]
        cum = d[0:chunk]
        total = cum[0:1, :] if reverse else cum[chunk - 1:chunk, :]
        k = k_ref[rows(c), :]
        prep = dict(decay=jnp.exp2(total), k_out=k * jnp.exp2(total - cum).astype(BF16))
        if q_ref is not None:
            q = q_ref[rows(c), :]
            prep["q_in"] = q * jnp.exp2(cum).astype(BF16)
            pairs = []
            for level in range(n_level):
                e = jnp.exp2(-jnp.abs(d[(level + 1) * chunk:(level + 2) * chunk])).astype(BF16)
                pairs.append((q * e, k * e))
            if base == 1:
                pairs.append((q, k))
            else:
                since = d[(n_level + 1) * chunk:(n_level + 2) * chunk]
                pairs.append((q * jnp.exp2(since).astype(BF16), k * jnp.exp2(-since).astype(BF16)))
            prep["pairs"] = pairs
        preps[c] = prep
    for c in order:
        prep = preps[c]
        prep["kv"] = _dot(v_ref[rows(c), :].astype(F32).T.astype(BF16), prep.pop("k_out"))
        if q_ref is not None:
            prep["terms"] = [_dot_nt(qe, ke) for qe, ke in prep.pop("pairs")]
    if q_ref is not None:
        for c in order:
            terms = preps[c].pop("terms")
            scores = masks[0] * terms[0]
            for level in range(1, len(terms)):
                scores = scores + masks[level] * terms[level]
            preps[c]["scores"] = scores.astype(BF16)
        for c in order:
            preps[c]["intra"] = _dot(preps[c].pop("scores"), v_ref[rows(c), :])
    return preps


def _chunk_apply(st, prep):
    st_new = st * prep["decay"] + prep["kv"]
    if "intra" not in prep:
        return st_new, None
    return st_new, _dot_nt(prep["q_in"], st.astype(BF16)) + prep["intra"]


def _scan_order(n, reverse):
    return list(range(n - 1, -1, -1) if reverse else range(n))


def _scan_state_kernel(k_ref, lf_ref, v_ref, s0_ref, dist_ref, sfin_ref, st_ref, *, chunk, reverse):
    step = pl.program_id(2)

    @pl.when(step == 0)
    def _():
        st_ref[...] = s0_ref[...]

    rows = lambda c: slice(c * chunk, (c + 1) * chunk)
    order = _scan_order(k_ref.shape[0] // chunk, reverse)
    preps = _block_prep(order, rows, dist_ref[...], reverse, 1, k_ref, lf_ref, v_ref)
    st = st_ref[...]
    for c in order:
        st, _ = _chunk_apply(st, preps[c])
    st_ref[...] = st

    @pl.when(step == pl.num_programs(2) - 1)
    def _():
        sfin_ref[...] = st


def _scan_out_kernel(*refs, chunk, reverse, final, base):
    if final:
        q_ref, k_ref, lf_ref, v_ref, s0_ref, dist_ref, masks_ref, of_ref, g_ref, gain_ref, o_ref, st_ref = refs
    else:
        q_ref, k_ref, lf_ref, v_ref, s0_ref, dist_ref, masks_ref, o_ref, st_ref = refs

    @pl.when(pl.program_id(2) == 0)
    def _():
        st_ref[...] = s0_ref[...]

    rows = lambda c: slice(c * chunk, (c + 1) * chunk)
    order = _scan_order(k_ref.shape[0] // chunk, reverse)
    preps = _block_prep(order, rows, dist_ref[...], reverse, base, k_ref, lf_ref, v_ref, q_ref, masks_ref[...])
    st = st_ref[...]
    for c in order:
        st, out = _chunk_apply(st, preps[c])
        if final:
            tot = out + of_ref[rows(c), :]
            ms = jnp.mean(tot * tot, axis=-1, keepdims=True)
            gate = g_ref[rows(c), :].astype(F32)
            out = tot * lax.rsqrt(ms + EPS) * gain_ref[...] * (gate * jax.nn.sigmoid(gate))
        o_ref[rows(c), :] = out.astype(o_ref.dtype)
    st_ref[...] = st


def _scan_geometry(t, blk_pref=4096, chunk_pref=128):
    chunk = _pick(t, chunk_pref, LANES)
    blk = _pick(t, blk_pref, chunk)
    return chunk, blk


def _scan_state(kf, lf, qig, s0, direction):
    bsz, t, _ = kf.shape
    n_h = s0.shape[1]
    reverse = direction == 1
    chunk, blk = _scan_geometry(t)
    n_blk = t // blk
    dist = _scan_consts(chunk, reverse, 1)[1][:chunk]
    pos = (lambda c: n_blk - 1 - c) if reverse else (lambda c: c)
    kcol = direction * n_h
    return pl.pallas_call(
        functools.partial(_scan_state_kernel, chunk=chunk, reverse=reverse),
        grid=(bsz, n_h, n_blk),
        in_specs=[pl.BlockSpec((None, blk, REC_DK), lambda b, h, c: (b, pos(c), kcol + h)),
                  pl.BlockSpec((None, blk, REC_DK), lambda b, h, c: (b, pos(c), kcol + h)),
                  pl.BlockSpec((None, blk, REC_DK), lambda b, h, c: (b, pos(c), n_h + h)),
                  pl.BlockSpec((None, None, REC_DK, REC_DK), lambda b, h, c: (b, h, 0, 0)),
                  pl.BlockSpec(dist.shape, lambda b, h, c: (0, 0))],
        out_specs=pl.BlockSpec((None, None, REC_DK, REC_DK), lambda b, h, c: (b, h, 0, 0)),
        out_shape=jax.ShapeDtypeStruct(s0.shape, F32),
        scratch_shapes=[pltpu.VMEM((REC_DK, REC_DK), F32)],
        compiler_params=_params(("arbitrary", "arbitrary", "arbitrary")),
        name="hgrn_state_bwd" if reverse else "hgrn_state_fwd",
    )(kf, lf, qig, s0, jnp.asarray(dist, BF16))


def _scan_out(kf, lf, qig, s0, direction, base, other=None, out_gain=None):
    bsz, t, _ = kf.shape
    n_h = s0.shape[1]
    reverse = direction == 1
    final = other is not None
    chunk, blk = _scan_geometry(t)
    n_blk = t // blk
    masks, dist = _scan_consts(chunk, reverse, base)
    pos = (lambda c: n_blk - 1 - c) if reverse else (lambda c: c)
    kcol = direction * n_h
    tok = lambda col: pl.BlockSpec((None, blk, REC_DK), lambda b, h, c: (b, pos(c), col + h))
    in_specs = [tok(0), tok(kcol), tok(kcol), tok(n_h),
                pl.BlockSpec((None, None, REC_DK, REC_DK), lambda b, h, c: (b, h, 0, 0)),
                pl.BlockSpec(dist.shape, lambda b, h, c: (0, 0)),
                pl.BlockSpec(masks.shape, lambda b, h, c: (0, 0, 0))]
    args = [qig, kf, lf, qig, s0, jnp.asarray(dist, BF16), jnp.asarray(masks)]
    if final:
        in_specs += [tok(0), tok(2 * n_h), pl.BlockSpec((1, REC_DK), lambda b, h, c: (0, 0))]
        args += [other, qig, out_gain.reshape(1, REC_DK)]
    return pl.pallas_call(
        functools.partial(_scan_out_kernel, chunk=chunk, reverse=reverse, final=final, base=base),
        grid=(bsz, n_h, n_blk),
        in_specs=in_specs,
        out_specs=tok(0),
        out_shape=jax.ShapeDtypeStruct((bsz, t, n_h * REC_DK), BF16 if final else F32),
        scratch_shapes=[pltpu.VMEM((REC_DK, REC_DK), F32)],
        compiler_params=_params(("arbitrary", "arbitrary", "arbitrary")),
        name="hgrn_scan_bwd" if reverse else "hgrn_scan_fwd",
    )(*args)


def _rope_tables(n_tokens):
    n_rows = n_tokens // GRID_W
    row = jnp.repeat(jnp.arange(n_rows, dtype=F32), GRID_W)
    col = jnp.tile(jnp.arange(GRID_W, dtype=F32), n_rows)
    quarter = HEAD_DIM // 4
    inv_freq = ROPE_THETA ** (-jnp.arange(quarter, dtype=F32) / quarter)
    ang = jnp.concatenate([row[:, None] * inv_freq, col[:, None] * inv_freq], axis=-1)
    cos, sin = jnp.cos(ang), jnp.sin(ang)
    return jnp.concatenate([cos, cos], axis=-1), jnp.concatenate([-sin, sin], axis=-1)


def _conv_ffn(x, gain, shift, scale, gate, w_up, conv_w, conv_b, w_down):
    act = _ffn_up(x, gain, shift, scale, w_up, conv_w, conv_b)
    return _mm_res([act], [w_down], x, gate)


def kernel(x, c, ctx, c_ctx, w_mod, b_mod, norm_gain, attn_w_in, attn_w_out, attn_qk_gain, attn_sink,
           rec_w_in, rec_w_out, rec_out_gain, rec_lb_gamma, ffn_w_up, ffn_conv_w, ffn_conv_b, ffn_w_down):
    bsz, n_lat, d = x.shape
    depth = w_mod.shape[0]
    assert depth == 2 and bsz < SUBLANES, "layer 0 attention, layer 1 recurrence; conditioning rows fit one tile"
    n_heads_rec = d // REC_DK

    cvec = jnp.zeros((SUBLANES, d), F32).at[:bsz].set(c).at[bsz].set(c_ctx)
    mods = _mods(cvec, w_mod, b_mod)

    def lat_mod(layer, k):
        return mods[layer, :bsz, k * d:(k + 1) * d][:, None, :]

    def ctx_mod(layer, k):
        return jnp.broadcast_to(mods[layer, bsz, k * d:(k + 1) * d][None, None, :], (bsz, 1, d))

    w_up = ffn_w_up.astype(BF16)
    w_down = ffn_w_down.astype(BF16)

    aq = N_Q_HEADS * HEAD_DIM
    akv = N_KV_HEADS * HEAD_DIM
    w_in = attn_w_in[0]
    qa, ka, va, qb, kb, vb = jnp.split(w_in, np.cumsum([aq, akv, akv, aq, akv])[:5].tolist(), axis=1)
    w_qk = jnp.concatenate([qa, qb, ka, kb], axis=1).astype(BF16)
    w_v = jnp.concatenate([va, vb], axis=1).astype(BF16)
    qkg = attn_qk_gain[0]
    q_scale = ATTN_SCALE * LOG2E
    head_gain = jnp.concatenate([jnp.tile(qkg[0] * q_scale, N_Q_HEADS), jnp.tile(qkg[2] * q_scale, N_Q_HEADS),
                                 jnp.tile(qkg[1], N_KV_HEADS), jnp.tile(qkg[3], N_KV_HEADS)])[None, :]
    cos, sin = _rope_tables(n_lat)
    g0 = norm_gain[0, 0]
    qk_l, v_l = _proj_qkv(x, g0, lat_mod(0, 0), lat_mod(0, 1), w_qk, w_v, head_gain, cos, sin)
    qk_c, v_c = _proj_qkv(ctx, g0, ctx_mod(0, 0), ctx_mod(0, 1), w_qk, w_v, head_gain)

    ka_cols = slice(2 * aq, 2 * aq + akv)
    k_all = jnp.concatenate([qk_c[:, :, ka_cols], qk_l[:, :, ka_cols]], axis=1)
    v_all = jnp.concatenate([v_c[:, :, :akv], v_l[:, :, :akv]], axis=1)
    ones = jnp.ones(v_all.shape[:2] + (HEAD_DIM,), BF16)
    v_ext = jnp.concatenate([v_all[:, :, :HEAD_DIM], ones, v_all[:, :, HEAD_DIM:], ones], axis=-1)
    sink = attn_sink[0].astype(F32)
    o_a = _attn_a(qk_l, k_all, v_ext)
    o_b = _attn_b(qk_l, v_l, qk_c, v_c, sink)
    o_c = _attn_ctx(qk_c, v_c, jnp.concatenate([jnp.full((N_Q_HEADS,), -jnp.inf, F32), sink]))

    w_out = attn_w_out[0].astype(BF16)
    x = _mm_res([o_a, o_b], [w_out[:aq], w_out[aq:]], x, lat_mod(0, 2))
    ctx = _mm_res([o_c], [w_out], ctx, ctx_mod(0, 2))
    g1 = norm_gain[0, 1]
    x = _conv_ffn(x, g1, lat_mod(0, 3), lat_mod(0, 4), lat_mod(0, 5), w_up[0], ffn_conv_w[0], ffn_conv_b[0], w_down[0])
    ctx = _conv_ffn(ctx, g1, ctx_mod(0, 3), ctx_mod(0, 4), ctx_mod(0, 5), w_up[0], ffn_conv_w[0], ffn_conv_b[0],
                    w_down[0])

    lb_soft = jax.nn.softmax(rec_lb_gamma.astype(F32), axis=1)
    lower = (jnp.cumsum(lb_soft, axis=1) - lb_soft[:, :1])[:, 1]
    rk = n_heads_rec * REC_DK
    w_q, w_ff, w_fb, w_i, w_g = jnp.split(rec_w_in[0], [rk, 2 * rk, 3 * rk, 3 * rk + d], axis=1)
    w_qig = jnp.concatenate([w_q, w_i, w_g], axis=1).astype(BF16)
    w_f = jnp.concatenate([w_ff, w_fb], axis=1).astype(BF16)
    lb = lower.reshape(1, 2 * rk)
    g0 = norm_gain[1, 0]
    lf_l, kf_l, qig_l = _proj_rec(x, g0, lat_mod(1, 0), lat_mod(1, 1), w_f, w_qig, lb)
    lf_c, kf_c, qig_c = _proj_rec(ctx, g0, ctx_mod(1, 0), ctx_mod(1, 1), w_f, w_qig, lb)

    s_zero = jnp.zeros((bsz, n_heads_rec, REC_DK, REC_DK), F32)
    s_fwd = _scan_state(kf_c, lf_c, qig_c, s_zero, 0)
    s_bwd = _scan_state(kf_c, lf_c, qig_c, s_zero, 1)

    def scans(base):
        def run(kf, lf, qig, s_f, s_b, gain):
            o_f = _scan_out(kf, lf, qig, s_f, 0, base)
            return _scan_out(kf, lf, qig, s_b, 1, base, other=o_f, out_gain=gain)
        return run

    worst_exp2 = FAST_BASE * jnp.max(-jnp.log2(lower))
    y = lax.cond(worst_exp2 <= MAX_EXP2, scans(FAST_BASE), scans(1),
                 kf_l, lf_l, qig_l, s_fwd, s_bwd, rec_out_gain[0])
    x = _mm_res([y], [rec_w_out[0].astype(BF16)], x, lat_mod(1, 2))
    g1 = norm_gain[1, 1]
    x = _conv_ffn(x, g1, lat_mod(1, 3), lat_mod(1, 4), lat_mod(1, 5), w_up[1], ffn_conv_w[1], ffn_conv_b[1], w_down[1])
    return x
```

```python
import functools
import math

import numpy as np
import jax
import jax.numpy as jnp
from jax import lax
from jax.experimental import pallas as pl
from jax.experimental.pallas import tpu as pltpu

F32 = jnp.float32
BF16 = jnp.bfloat16

HEAD_DIM = 128
GRID_W = 64
N_Q_HEADS = 8
N_KV_HEADS = 2
GROUP = N_Q_HEADS // N_KV_HEADS
WINDOW = 128
ROPE_THETA = 10000.0
ATTN_SCALE = HEAD_DIM ** -0.5
NEG_INF = -1e30
LOG2E = 1.4426950408889634
REC_DK = 128
FAST_BASE = 64
MAX_EXP2 = 100.0
N_MOD = 6
EPS = 1e-6

LANES = 128
SUBLANES = 8
BF16_ROWS = 16
MXU_COLS = 256
CHAINS_PER_STAGE = 4
VMEM_LIMIT = 56 * 1024 * 1024
W_TILE_BYTES = 6 * 1024 * 1024


def _pick(n, pref, align):
    if n <= pref:
        return n
    t = (pref // align) * align
    while t >= align:
        if n % t == 0:
            return t
        t -= align
    raise ValueError(f"no tile for {n} (pref {pref}, align {align})")


def _params(sem):
    return pltpu.CompilerParams(dimension_semantics=sem, vmem_limit_bytes=VMEM_LIMIT)


def _dot(a, b):
    return jnp.dot(a, b, preferred_element_type=F32)


def _dot_nt(a, b):
    return lax.dot_general(a, b, (((1,), (1,)), ((), ())), preferred_element_type=F32)


def _mod_kernel(c_ref, w_ref, b_ref, o_ref):
    cv = c_ref[...]
    o_ref[...] = _dot(cv * jax.nn.sigmoid(cv), w_ref[...]) + b_ref[...]


def _mods(cvec, w_mod, b_mod):
    depth, d, n = w_mod.shape
    tn = _pick(n, 1024, LANES)
    return pl.pallas_call(
        _mod_kernel,
        grid=(depth, n // tn),
        in_specs=[pl.BlockSpec((SUBLANES, d), lambda l, j: (0, 0)),
                  pl.BlockSpec((None, d, tn), lambda l, j: (l, 0, j)),
                  pl.BlockSpec((None, 1, tn), lambda l, j: (l, 0, j))],
        out_specs=pl.BlockSpec((None, SUBLANES, tn), lambda l, j: (l, 0, j)),
        out_shape=jax.ShapeDtypeStruct((depth, SUBLANES, n), F32),
        compiler_params=_params(("arbitrary", "arbitrary")),
        name="adaln_mods",
    )(cvec, w_mod, b_mod.reshape(depth, 1, n))


def _norm_mod(x, g, sh, sc):
    ms = jnp.mean(x * x, axis=-1, keepdims=True)
    return (x * lax.rsqrt(ms + EPS)) * (g * (1.0 + sc)) + sh


def _fill_h(x_ref, g_ref, sh_ref, sc_ref, h_ref):
    @pl.when(pl.program_id(2) == 0)
    def _():
        h_ref[...] = _norm_mod(x_ref[...], g_ref[...], sh_ref[...], sc_ref[...]).astype(BF16)


def _plain_blocks(h, w_ref, start, width):
    return [_dot(h, w_ref[:, start + c0:start + min(c0 + MXU_COLS, width)]) for c0 in range(0, width, MXU_COLS)]


def _store_blocks(o_ref, blocks):
    c0 = 0
    for blk in blocks:
        o_ref[:, c0:c0 + blk.shape[1]] = blk.astype(o_ref.dtype)
        c0 += blk.shape[1]


def _proj_qkv_kernel(*refs, rope):
    if rope:
        x_ref, g_ref, sh_ref, sc_ref, w_ref, hg_ref, cos_ref, sin_ref, qk_ref, v_ref, h_ref = refs
    else:
        x_ref, g_ref, sh_ref, sc_ref, w_ref, hg_ref, qk_ref, v_ref, h_ref = refs
    _fill_h(x_ref, g_ref, sh_ref, sc_ref, h_ref)
    h = h_ref[...]
    n_qk = qk_ref.shape[1]
    accs = _plain_blocks(h, w_ref, 0, n_qk)
    v_blocks = _plain_blocks(h, w_ref, n_qk, v_ref.shape[1])
    heads = [acc[:, c:c + HEAD_DIM] for acc in accs for c in range(0, acc.shape[1], HEAD_DIM)]
    cols = [slice(c, c + HEAD_DIM) for c in range(0, n_qk, HEAD_DIM)]
    inv = [lax.rsqrt(jnp.mean(x * x, axis=-1, keepdims=True) + EPS) for x in heads]
    ys = [x * r * hg_ref[:, c] for x, r, c in zip(heads, inv, cols)]
    if rope:
        ys = [y * cos_ref[...] + pltpu.roll(y, HEAD_DIM // 2, 1) * sin_ref[...] for y in ys]
    for y, c in zip(ys, cols):
        qk_ref[:, c] = y.astype(qk_ref.dtype)
    _store_blocks(v_ref, v_blocks)


def _proj_rec_kernel(x_ref, g_ref, sh_ref, sc_ref, w_ref, lb_ref, lf_ref, k_ref, qig_ref, h_ref):
    _fill_h(x_ref, g_ref, sh_ref, sc_ref, h_ref)
    h = h_ref[...]
    n_f = lf_ref.shape[1]
    sigs = [jax.nn.sigmoid(z) for z in _plain_blocks(h, w_ref, 0, n_f)]
    plain = _plain_blocks(h, w_ref, n_f, qig_ref.shape[1])
    c0 = 0
    for sig in sigs:
        c = slice(c0, c0 + sig.shape[1])
        lb = lb_ref[:, c]
        lf_ref[:, c] = jnp.log2(lb + (1.0 - lb) * sig)
        k_ref[:, c] = ((1.0 - lb) * (1.0 - sig)).astype(k_ref.dtype)
        c0 += sig.shape[1]
    _store_blocks(qig_ref, plain)


def _interleave(w_a, w_b, n_step):
    ta, tb = w_a.shape[1] // n_step, w_b.shape[1] // n_step
    tiles = []
    for j in range(n_step):
        tiles += [w_a[:, j * ta:(j + 1) * ta], w_b[:, j * tb:(j + 1) * tb]]
    return jnp.concatenate(tiles, axis=1)


def _proj_call(kernel, name, x, gain, shift, scale, w, n_step, extras, extra_specs, outs, tm_pref=1024):
    bsz, t, d = x.shape
    tm = _pick(t, tm_pref, BF16_ROWS)
    in_specs = [pl.BlockSpec((None, tm, d), lambda b, i, j: (b, i, 0)),
                pl.BlockSpec((1, d), lambda b, i, j: (0, 0)),
                pl.BlockSpec((None, 1, d), lambda b, i, j: (b, 0, 0)),
                pl.BlockSpec((None, 1, d), lambda b, i, j: (b, 0, 0)),
                pl.BlockSpec((d, w.shape[1] // n_step), lambda b, i, j: (0, j))] + [s(tm) for s in extra_specs]
    return pl.pallas_call(
        kernel,
        grid=(bsz, t // tm, n_step),
        in_specs=in_specs,
        out_specs=[pl.BlockSpec((None, tm, n // n_step), lambda b, i, j: (b, i, j)) for n, _ in outs],
        out_shape=[jax.ShapeDtypeStruct((bsz, t, n), dt) for n, dt in outs],
        scratch_shapes=[pltpu.VMEM((tm, d), BF16)],
        compiler_params=_params(("arbitrary", "arbitrary", "arbitrary")),
        name=name,
    )(x, gain.reshape(1, d), shift, scale, w, *extras)


def _col_spec(width):
    return lambda tm: pl.BlockSpec((1, width), lambda b, i, j: (0, j))


def _pos_spec(tm):
    return pl.BlockSpec((tm, HEAD_DIM), lambda b, i, j: (i, 0))


def _n_steps(n_a, n_b):
    return math.gcd(n_a // MXU_COLS, n_b // MXU_COLS)


def _proj_qkv(x, gain, shift, scale, w_qk, w_v, head_gain, cos=None, sin=None):
    n_qk, n_v = w_qk.shape[1], w_v.shape[1]
    n_step = _n_steps(n_qk, n_v)
    w = _interleave(w_qk, w_v, n_step).astype(BF16)
    outs = ((n_qk, BF16), (n_v, BF16))
    hg_spec = _col_spec(n_qk // n_step)
    if cos is None:
        return _proj_call(functools.partial(_proj_qkv_kernel, rope=False), "proj_qkv_ctx", x, gain, shift, scale, w,
                          n_step, (head_gain,), (hg_spec,), outs)
    return _proj_call(functools.partial(_proj_qkv_kernel, rope=True), "proj_qkv_rope", x, gain, shift, scale, w,
                      n_step, (head_gain, cos, sin), (hg_spec, _pos_spec, _pos_spec), outs)


def _proj_rec(x, gain, shift, scale, w_f, w_qig, lb):
    n_f, n_p = w_f.shape[1], w_qig.shape[1]
    n_step = _n_steps(n_f, n_p)
    outs = ((n_f, F32), (n_f, BF16), (n_p, BF16))
    w = _interleave(w_f, w_qig, n_step).astype(BF16)
    return _proj_call(_proj_rec_kernel, "proj_rec", x, gain, shift, scale, w, n_step,
                      (lb,), (_col_spec(n_f // n_step),), outs)


def _mm_res_kernel(a_ref, w_ref, r_ref, gt_ref, o_ref):
    o_ref[...] = r_ref[...] + gt_ref[...] * _dot(a_ref[...], w_ref[...])


def _mm2_res_kernel(a1_ref, a2_ref, w1_ref, w2_ref, r_ref, gt_ref, o_ref):
    acc = _dot(a1_ref[...], w1_ref[...]) + _dot(a2_ref[...], w2_ref[...])
    o_ref[...] = r_ref[...] + gt_ref[...] * acc


def _mm_res(acts, ws, res, gate, tm_pref=1024):
    bsz, t, n = res.shape
    tm = _pick(t, tm_pref, BF16_ROWS)
    k_total = sum(w.shape[0] for w in ws)
    tn = _pick(n, max(LANES, W_TILE_BYTES // (2 * k_total)), LANES)
    kernel = _mm_res_kernel if len(acts) == 1 else _mm2_res_kernel
    in_specs = ([pl.BlockSpec((None, tm, a.shape[2]), lambda b, i, j: (b, i, 0)) for a in acts]
                + [pl.BlockSpec((w.shape[0], tn), lambda b, i, j: (0, j)) for w in ws]
                + [pl.BlockSpec((None, tm, tn), lambda b, i, j: (b, i, j)),
                   pl.BlockSpec((None, 1, tn), lambda b, i, j: (b, 0, j))])
    return pl.pallas_call(
        kernel,
        grid=(bsz, t // tm, n // tn),
        in_specs=in_specs,
        out_specs=pl.BlockSpec((None, tm, tn), lambda b, i, j: (b, i, j)),
        out_shape=jax.ShapeDtypeStruct((bsz, t, n), F32),
        compiler_params=_params(("arbitrary", "arbitrary", "arbitrary")),
        name="mm_res" if len(acts) == 1 else "mm2_res",
    )(*acts, *ws, res, gate)


def _ffn_up_kernel(xp_ref, x_ref, xn_ref, g_ref, sh_ref, sc_ref, wg_ref, wv_ref, cwg_ref, cwv_ref,
                   cbg_ref, cbv_ref, o_ref, h_ref, *, tm, halo):
    i = pl.program_id(1)
    last = pl.num_programs(1) - 1

    @pl.when(pl.program_id(2) == 0)
    def _():
        g, sh, sc = g_ref[...], sh_ref[...], sc_ref[...]
        hp = jnp.where(i > 0, _norm_mod(xp_ref[...], g, sh, sc), 0.0)
        hn = jnp.where(i < last, _norm_mod(xn_ref[...], g, sh, sc), 0.0)
        h_ref[0:halo, :] = hp.astype(BF16)
        h_ref[halo:halo + tm, :] = _norm_mod(x_ref[...], g, sh, sc).astype(BF16)
        h_ref[halo + tm:, :] = hn.astype(BF16)

    h = h_ref[...]
    rows = tm + 2 * halo

    def branch(w_ref, cw_ref, cb_ref):
        u = _dot(h, w_ref[...])
        u_prev = pltpu.roll(u, 1, 0)[halo:halo + tm]
        u_next = pltpu.roll(u, rows - 1, 0)[halo:halo + tm]
        cw = cw_ref[...]
        return u_prev * cw[0:1] + u[halo:halo + tm] * cw[1:2] + u_next * cw[2:3] + cb_ref[...]

    yg = branch(wg_ref, cwg_ref, cbg_ref)
    yv = branch(wv_ref, cwv_ref, cbv_ref)
    o_ref[...] = (yg * jax.nn.sigmoid(yg) * yv).astype(o_ref.dtype)


def _ffn_up(x, gain, shift, scale, w_up, conv_w, conv_b, tm_pref=1024, tn_pref=512):
    bsz, t, d = x.shape
    d_ff = w_up.shape[1] // 2
    halo = BF16_ROWS
    tm = _pick(t, tm_pref, halo)
    tn = _pick(d_ff, tn_pref, LANES)
    nj = d_ff // tn
    r = tm // halo
    n_halo = t // halo
    kern = functools.partial(_ffn_up_kernel, tm=tm, halo=halo)
    conv_b = conv_b.reshape(1, 2 * d_ff)
    vec = lambda b, i, j: (b, 0, 0)
    return pl.pallas_call(
        kern,
        grid=(bsz, t // tm, nj),
        in_specs=[pl.BlockSpec((None, halo, d), lambda b, i, j: (b, jnp.maximum(i * r - 1, 0), 0)),
                  pl.BlockSpec((None, tm, d), lambda b, i, j: (b, i, 0)),
                  pl.BlockSpec((None, halo, d), lambda b, i, j: (b, jnp.minimum((i + 1) * r, n_halo - 1), 0)),
                  pl.BlockSpec((1, d), lambda b, i, j: (0, 0)),
                  pl.BlockSpec((None, 1, d), vec),
                  pl.BlockSpec((None, 1, d), vec),
                  pl.BlockSpec((d, tn), lambda b, i, j: (0, j)),
                  pl.BlockSpec((d, tn), lambda b, i, j: (0, j + nj)),
                  pl.BlockSpec((conv_w.shape[0], tn), lambda b, i, j: (0, j)),
                  pl.BlockSpec((conv_w.shape[0], tn), lambda b, i, j: (0, j + nj)),
                  pl.BlockSpec((1, tn), lambda b, i, j: (0, j)),
                  pl.BlockSpec((1, tn), lambda b, i, j: (0, j + nj))],
        out_specs=pl.BlockSpec((None, tm, tn), lambda b, i, j: (b, i, j)),
        out_shape=jax.ShapeDtypeStruct((bsz, t, d_ff), BF16),
        scratch_shapes=[pltpu.VMEM((tm + 2 * halo, d), BF16)],
        compiler_params=_params(("arbitrary", "arbitrary", "arbitrary")),
        name="ffn_up_conv",
    )(x, x, x, gain.reshape(1, d), shift, scale, w_up, w_up, conv_w, conv_w, conv_b, conv_b)


def _attn_a_kernel(q_ref, k_ref, v_ref, o_ref, m_ref, acc_ref, *, tq, rs):
    step = pl.program_id(3)

    @pl.when(step == 0)
    def _():
        m_ref[...] = jnp.full(m_ref.shape, -jnp.inf, F32)
        acc_ref[...] = jnp.zeros(acc_ref.shape, F32)

    k = k_ref[...]
    v = v_ref[...]
    n_chunk = k.shape[0] // LANES
    chains = [(g, r) for g in range(GROUP) for r in range(tq // rs)]
    for start in range(0, len(chains), CHAINS_PER_STAGE):
        group = chains[start:start + CHAINS_PER_STAGE]
        rows = [slice(g * tq + r * rs, g * tq + (r + 1) * rs) for g, r in group]
        scores = [_dot_nt(q_ref[r * rs:(r + 1) * rs, g * HEAD_DIM:(g + 1) * HEAD_DIM], k) for g, r in group]
        chunks = [[s[:, c * LANES:(c + 1) * LANES] for c in range(n_chunk)] for s in scores]
        m_prev = [m_ref[rw, :] for rw in rows]
        m_new = [jnp.maximum(mp, jnp.max(functools.reduce(jnp.maximum, ch), axis=-1, keepdims=True))
                 for mp, ch in zip(m_prev, chunks)]
        probs = [jnp.concatenate([jnp.exp2(c - mn) for c in ch], axis=1).astype(BF16) for mn, ch in zip(m_new, chunks)]
        pv = [_dot(p, v) for p in probs]
        for rw, mp, mn, upd in zip(rows, m_prev, m_new, pv):
            alpha = jnp.exp2(mp - mn)
            acc_ref[rw, :] = jnp.concatenate([alpha, alpha], axis=1) * acc_ref[rw, :] + upd
            m_ref[rw, :] = mn

    @pl.when(step == pl.num_programs(3) - 1)
    def _():
        for g in range(GROUP):
            acc = acc_ref[g * tq:(g + 1) * tq, :]
            o_ref[:, g * HEAD_DIM:(g + 1) * HEAD_DIM] = (acc[:, :HEAD_DIM] / acc[:, HEAD_DIM:]).astype(o_ref.dtype)


def _attn_a(qk, k_all, v_ext, tq_pref=1024, tk_pref=3328, rs_pref=256):
    bsz, t, _ = qk.shape
    s = k_all.shape[1]
    tq = _pick(t, tq_pref, BF16_ROWS)
    rs = _pick(tq, rs_pref, BF16_ROWS)
    tk = _pick(s, tk_pref, LANES)
    gw = GROUP * HEAD_DIM
    m = GROUP * tq
    return pl.pallas_call(
        functools.partial(_attn_a_kernel, tq=tq, rs=rs),
        grid=(bsz, N_KV_HEADS, t // tq, s // tk),
        in_specs=[pl.BlockSpec((None, tq, gw), lambda b, h, i, j: (b, i, h)),
                  pl.BlockSpec((None, tk, HEAD_DIM), lambda b, h, i, j: (b, j, h)),
                  pl.BlockSpec((None, tk, 2 * HEAD_DIM), lambda b, h, i, j: (b, j, h))],
        out_specs=pl.BlockSpec((None, tq, gw), lambda b, h, i, j: (b, i, h)),
        out_shape=jax.ShapeDtypeStruct((bsz, t, N_KV_HEADS * gw), BF16),
        scratch_shapes=[pltpu.VMEM((m, HEAD_DIM), F32), pltpu.VMEM((m, 2 * HEAD_DIM), F32)],
        compiler_params=_params(("arbitrary", "arbitrary", "arbitrary", "arbitrary")),
        name="attn_global",
    )(qk, k_all, v_ext)


def _softmax_out(s, sink, v):
    sink = sink * LOG2E
    m = jnp.maximum(jnp.max(s, axis=-1, keepdims=True), sink)
    p = jnp.exp2(s - m)
    den = jnp.sum(p, axis=-1, keepdims=True) + jnp.exp2(sink - m)
    return _dot(p.astype(BF16), v) / den


def _attn_b_kernel(sink_ref, bias_ref, q_ref, kc_ref, vc_ref, kp_ref, km_ref, kn_ref, vp_ref, vm_ref, vn_ref,
                   o_ref, kcat_ref, vcat_ref):
    kv = pl.program_id(1)
    for dst, parts in ((kcat_ref, (kc_ref, kp_ref, km_ref, kn_ref)), (vcat_ref, (vc_ref, vp_ref, vm_ref, vn_ref))):
        off = 0
        for part in parts:
            dst[off:off + part.shape[0], :] = part[...]
            off += part.shape[0]
    kcat = kcat_ref[...]
    vcat = vcat_ref[...]
    bias = bias_ref[...]
    heads = range(GROUP)
    cols = [slice(g * HEAD_DIM, (g + 1) * HEAD_DIM) for g in heads]
    scores = [_dot_nt(q_ref[:, cols[g]], kcat) + bias for g in heads]
    sinks = [sink_ref[kv * GROUP + g] * LOG2E for g in heads]
    tops = [jnp.maximum(jnp.max(scores[g], axis=-1, keepdims=True), sinks[g]) for g in heads]
    probs = [jnp.exp2(scores[g] - tops[g]) for g in heads]
    dens = [jnp.sum(probs[g], axis=-1, keepdims=True) + jnp.exp2(sinks[g] - tops[g]) for g in heads]
    outs = [_dot(probs[g].astype(BF16), vcat) for g in heads]
    for g in heads:
        o_ref[:, cols[g]] = (outs[g] / dens[g]).astype(o_ref.dtype)


def _attn_b(qk, v, qk_c, v_c, sink, tq_pref=256):
    bsz, t, _ = qk.shape
    n_ctx = qk_c.shape[1]
    tq = _pick(t, tq_pref, WINDOW)
    r = tq // WINDOW
    n_w = t // WINDOW
    gw = GROUP * HEAD_DIM
    kb, vb = 18, 2
    prev = lambda c: (lambda b, h, i: (b, jnp.maximum(i * r - 1, 0), c + h))
    main = lambda c: (lambda b, h, i: (b, i, c + h))
    nxt = lambda c: (lambda b, h, i: (b, jnp.minimum((i + 1) * r, n_w - 1), c + h))
    nk = n_ctx + tq + 2 * WINDOW
    n_i = t // tq
    row = np.arange(tq)[:, None]
    col = np.arange(nk)[None, :]
    band = (col < n_ctx) | (np.abs(col - n_ctx - WINDOW - row) <= WINDOW)
    before = (col >= n_ctx) & (col < n_ctx + WINDOW)
    after = col >= n_ctx + WINDOW + tq
    bias = np.stack([np.where(band & ~(before & bool(v & 1)) & ~(after & bool(v & 2)), 0.0, NEG_INF)
                     for v in range(4)]).astype(np.float32)
    variant = lambda i: jnp.where(i == 0, 1, 0) + jnp.where(i == n_i - 1, 2, 0)
    return pl.pallas_call(
        _attn_b_kernel,
        grid=(bsz, N_KV_HEADS, n_i),
        in_specs=[pl.BlockSpec(memory_space=pltpu.SMEM),
                  pl.BlockSpec((None, tq, nk), lambda b, h, i: (variant(i), 0, 0)),
                  pl.BlockSpec((None, tq, gw), lambda b, h, i: (b, i, 2 + h)),
                  pl.BlockSpec((None, n_ctx, HEAD_DIM), lambda b, h, i: (b, 0, kb + h)),
                  pl.BlockSpec((None, n_ctx, HEAD_DIM), lambda b, h, i: (b, 0, vb + h)),
                  pl.BlockSpec((None, WINDOW, HEAD_DIM), prev(kb)),
                  pl.BlockSpec((None, tq, HEAD_DIM), main(kb)),
                  pl.BlockSpec((None, WINDOW, HEAD_DIM), nxt(kb)),
                  pl.BlockSpec((None, WINDOW, HEAD_DIM), prev(vb)),
                  pl.BlockSpec((None, tq, HEAD_DIM), main(vb)),
                  pl.BlockSpec((None, WINDOW, HEAD_DIM), nxt(vb))],
        out_specs=pl.BlockSpec((None, tq, gw), lambda b, h, i: (b, i, h)),
        out_shape=jax.ShapeDtypeStruct((bsz, t, N_KV_HEADS * gw), BF16),
        scratch_shapes=[pltpu.VMEM((nk, HEAD_DIM), BF16), pltpu.VMEM((nk, HEAD_DIM), BF16)],
        compiler_params=_params(("arbitrary", "arbitrary", "arbitrary")),
        name="attn_window",
    )(sink, jnp.asarray(bias), qk, qk_c, v_c, qk, qk, qk, v, v, v)


def _attn_ctx_kernel(sink_ref, q_ref, k_ref, v_ref, o_ref):
    grp = pl.program_id(1)
    k = k_ref[...]
    v = v_ref[...]
    for g in range(GROUP):
        cols = slice(g * HEAD_DIM, (g + 1) * HEAD_DIM)
        s = _dot_nt(q_ref[:, cols], k)
        o_ref[:, cols] = _softmax_out(s, sink_ref[grp * GROUP + g], v).astype(o_ref.dtype)


def _attn_ctx(qk_c, v_c, sink_all):
    bsz, n_ctx, _ = qk_c.shape
    gw = GROUP * HEAD_DIM
    n_grp = 2 * N_KV_HEADS
    return pl.pallas_call(
        _attn_ctx_kernel,
        grid=(bsz, n_grp),
        in_specs=[pl.BlockSpec(memory_space=pltpu.SMEM),
                  pl.BlockSpec((None, n_ctx, gw), lambda b, h: (b, 0, h)),
                  pl.BlockSpec((None, n_ctx, HEAD_DIM), lambda b, h: (b, 0, 16 + h)),
                  pl.BlockSpec((None, n_ctx, HEAD_DIM), lambda b, h: (b, 0, h))],
        out_specs=pl.BlockSpec((None, n_ctx, gw), lambda b, h: (b, 0, h)),
        out_shape=jax.ShapeDtypeStruct((bsz, n_ctx, n_grp * gw), BF16),
        compiler_params=_params(("arbitrary", "arbitrary")),
        name="attn_ctx",
    )(sink_all, qk_c, qk_c, v_c)


def _scan_consts(chunk, reverse, base):
    idx = np.arange(chunk)
    t = idx[:, None]
    s = idx[None, :]
    tri = ((s >= t) if reverse else (s <= t)).astype(np.float32)
    blocks = [tri]
    masks = []
    h = chunk // 2
    while h >= base:
        same = (t // (2 * h)) == (s // (2 * h))
        t_hi = (t // h) % 2 == 1
        s_hi = (s // h) % 2 == 1
        masks.append(same & (~t_hi & s_hi if reverse else t_hi & ~s_hi))
        ref = (idx // (2 * h)) * (2 * h) + (h if reverse else h - 1)
        blocks.append(tri - tri[ref, :])
        h //= 2
    same = (t // base) == (s // base)
    masks.append(same & ((s >= t) if reverse else (s <= t)))
    if base > 1:
        blocks.append(tri * same)
    dist = np.concatenate(blocks, axis=0)
    return np.stack(masks).astype(np.float32), np.concatenate([dist, dist], axis=1)


def _block_prep(order, rows, dist, reverse, base, k_ref, lf_ref, v_ref, q_ref=None, masks=None):
    chunk = rows(order[0]).stop - rows(order[0]).start
    n_level = (chunk // base).bit_length() - 1
    dists = {}
    for pos in range(0, len(order), 2):
        pair = order[pos:pos + 2]
        cols = []
        for c in pair:
            lf = lf_ref[rows(c), :]
            hi = lf.astype(BF16)
            cols.append(jnp.concatenate([hi, (lf - hi.astype(F32)).astype(BF16)], axis=0))
        dd = _dot(dist, cols[0] if len(cols) == 1 else jnp.concatenate(cols, axis=1))
        for i, c in enumerate(pair):
            dists[c] = dd[:, i * REC_DK:(i + 1) * REC_DK]
    preps = {}
    for c in order:
        d = dists[c]
        cum = d[0:chunk]
        total = cum[0:1, :] if reverse else cum[chunk - 1:chunk, :]
        k = k_ref[rows(c), :]
        prep = dict(decay=jnp.exp2(total), k_out=k * jnp.exp2(total - cum).astype(BF16))
        if q_ref is not None:
            q = q_ref[rows(c), :]
            prep["q_in"] = q * jnp.exp2(cum).astype(BF16)
            pairs = []
            for level in range(n_level):
                e = jnp.exp2(-jnp.abs(d[(level + 1) * chunk:(level + 2) * chunk])).astype(BF16)
                pairs.append((q * e, k * e))
            if base == 1:
                pairs.append((q, k))
            else:
                since = d[(n_level + 1) * chunk:(n_level + 2) * chunk]
                pairs.append((q * jnp.exp2(since).astype(BF16), k * jnp.exp2(-since).astype(BF16)))
            prep["pairs"] = pairs
        preps[c] = prep
    for c in order:
        prep = preps[c]
        prep["kv"] = _dot(v_ref[rows(c), :].astype(F32).T.astype(BF16), prep.pop("k_out"))
        if q_ref is not None:
            prep["terms"] = [_dot_nt(qe, ke) for qe, ke in prep.pop("pairs")]
    if q_ref is not None:
        for c in order:
            terms = preps[c].pop("terms")
            scores = masks[0] * terms[0]
            for level in range(1, len(terms)):
                scores = scores + masks[level] * terms[level]
            preps[c]["scores"] = scores.astype(BF16)
        for c in order:
            preps[c]["intra"] = _dot(preps[c].pop("scores"), v_ref[rows(c), :])
    return preps


def _chunk_apply(st, prep):
    st_new = st * prep["decay"] + prep["kv"]
    if "intra" not in prep:
        return st_new, None
    return st_new, _dot_nt(prep["q_in"], st.astype(BF16)) + prep["intra"]


def _scan_order(n, reverse):
    return list(range(n - 1, -1, -1) if reverse else range(n))


def _scan_state_kernel(k_ref, lf_ref, v_ref, s0_ref, dist_ref, sfin_ref, st_ref, *, chunk, reverse):
    step = pl.program_id(2)

    @pl.when(step == 0)
    def _():
        st_ref[...] = s0_ref[...]

    rows = lambda c: slice(c * chunk, (c + 1) * chunk)
    order = _scan_order(k_ref.shape[0] // chunk, reverse)
    preps = _block_prep(order, rows, dist_ref[...], reverse, 1, k_ref, lf_ref, v_ref)
    st = st_ref[...]
    for c in order:
        st, _ = _chunk_apply(st, preps[c])
    st_ref[...] = st

    @pl.when(step == pl.num_programs(2) - 1)
    def _():
        sfin_ref[...] = st


def _scan_out_kernel(*refs, chunk, reverse, final, base):
    if final:
        q_ref, k_ref, lf_ref, v_ref, s0_ref, dist_ref, masks_ref, of_ref, g_ref, gain_ref, o_ref, st_ref = refs
    else:
        q_ref, k_ref, lf_ref, v_ref, s0_ref, dist_ref, masks_ref, o_ref, st_ref = refs

    @pl.when(pl.program_id(2) == 0)
    def _():
        st_ref[...] = s0_ref[...]

    rows = lambda c: slice(c * chunk, (c + 1) * chunk)
    order = _scan_order(k_ref.shape[0] // chunk, reverse)
    preps = _block_prep(order, rows, dist_ref[...], reverse, base, k_ref, lf_ref, v_ref, q_ref, masks_ref[...])
    st = st_ref[...]
    for c in order:
        st, out = _chunk_apply(st, preps[c])
        if final:
            tot = out + of_ref[rows(c), :]
            ms = jnp.mean(tot * tot, axis=-1, keepdims=True)
            gate = g_ref[rows(c), :].astype(F32)
            out = tot * lax.rsqrt(ms + EPS) * gain_ref[...] * (gate * jax.nn.sigmoid(gate))
        o_ref[rows(c), :] = out.astype(o_ref.dtype)
    st_ref[...] = st


def _scan_geometry(t, blk_pref=4096, chunk_pref=128):
    chunk = _pick(t, chunk_pref, LANES)
    blk = _pick(t, blk_pref, chunk)
    return chunk, blk


def _scan_state(kf, lf, qig, s0, direction):
    bsz, t, _ = kf.shape
    n_h = s0.shape[1]
    reverse = direction == 1
    chunk, blk = _scan_geometry(t)
    n_blk = t // blk
    dist = _scan_consts(chunk, reverse, 1)[1][:chunk]
    pos = (lambda c: n_blk - 1 - c) if reverse else (lambda c: c)
    kcol = direction * n_h
    return pl.pallas_call(
        functools.partial(_scan_state_kernel, chunk=chunk, reverse=reverse),
        grid=(bsz, n_h, n_blk),
        in_specs=[pl.BlockSpec((None, blk, REC_DK), lambda b, h, c: (b, pos(c), kcol + h)),
                  pl.BlockSpec((None, blk, REC_DK), lambda b, h, c: (b, pos(c), kcol + h)),
                  pl.BlockSpec((None, blk, REC_DK), lambda b, h, c: (b, pos(c), n_h + h)),
                  pl.BlockSpec((None, None, REC_DK, REC_DK), lambda b, h, c: (b, h, 0, 0)),
                  pl.BlockSpec(dist.shape, lambda b, h, c: (0, 0))],
        out_specs=pl.BlockSpec((None, None, REC_DK, REC_DK), lambda b, h, c: (b, h, 0, 0)),
        out_shape=jax.ShapeDtypeStruct(s0.shape, F32),
        scratch_shapes=[pltpu.VMEM((REC_DK, REC_DK), F32)],
        compiler_params=_params(("arbitrary", "arbitrary", "arbitrary")),
        name="hgrn_state_bwd" if reverse else "hgrn_state_fwd",
    )(kf, lf, qig, s0, jnp.asarray(dist, BF16))


def _scan_out(kf, lf, qig, s0, direction, base, other=None, out_gain=None):
    bsz, t, _ = kf.shape
    n_h = s0.shape[1]
    reverse = direction == 1
    final = other is not None
    chunk, blk = _scan_geometry(t)
    n_blk = t // blk
    masks, dist = _scan_consts(chunk, reverse, base)
    pos = (lambda c: n_blk - 1 - c) if reverse else (lambda c: c)
    kcol = direction * n_h
    tok = lambda col: pl.BlockSpec((None, blk, REC_DK), lambda b, h, c: (b, pos(c), col + h))
    in_specs = [tok(0), tok(kcol), tok(kcol), tok(n_h),
                pl.BlockSpec((None, None, REC_DK, REC_DK), lambda b, h, c: (b, h, 0, 0)),
                pl.BlockSpec(dist.shape, lambda b, h, c: (0, 0)),
                pl.BlockSpec(masks.shape, lambda b, h, c: (0, 0, 0))]
    args = [qig, kf, lf, qig, s0, jnp.asarray(dist, BF16), jnp.asarray(masks)]
    if final:
        in_specs += [tok(0), tok(2 * n_h), pl.BlockSpec((1, REC_DK), lambda b, h, c: (0, 0))]
        args += [other, qig, out_gain.reshape(1, REC_DK)]
    return pl.pallas_call(
        functools.partial(_scan_out_kernel, chunk=chunk, reverse=reverse, final=final, base=base),
        grid=(bsz, n_h, n_blk),
        in_specs=in_specs,
        out_specs=tok(0),
        out_shape=jax.ShapeDtypeStruct((bsz, t, n_h * REC_DK), BF16 if final else F32),
        scratch_shapes=[pltpu.VMEM((REC_DK, REC_DK), F32)],
        compiler_params=_params(("arbitrary", "arbitrary", "arbitrary")),
        name="hgrn_scan_bwd" if reverse else "hgrn_scan_fwd",
    )(*args)


def _rope_tables(n_tokens):
    n_rows = n_tokens // GRID_W
    quarter = HEAD_DIM // 4
    inv_freq = ROPE_THETA ** (-jnp.arange(quarter, dtype=F32) / quarter)
    row_ang = jnp.arange(n_rows, dtype=F32)[:, None] * inv_freq
    col_ang = jnp.arange(GRID_W, dtype=F32)[:, None] * inv_freq

    def table(fn):
        by_row = jnp.broadcast_to(fn(row_ang)[:, None, :], (n_rows, GRID_W, quarter))
        by_col = jnp.broadcast_to(fn(col_ang)[None, :, :], (n_rows, GRID_W, quarter))
        return jnp.concatenate([by_row, by_col], axis=-1).reshape(n_tokens, 2 * quarter)

    cos, sin = table(jnp.cos), table(jnp.sin)
    return jnp.concatenate([cos, cos], axis=-1), jnp.concatenate([-sin, sin], axis=-1)


def _conv_ffn(x, gain, shift, scale, gate, w_up, conv_w, conv_b, w_down):
    act = _ffn_up(x, gain, shift, scale, w_up, conv_w, conv_b)
    return _mm_res([act], [w_down], x, gate)


def kernel(x, c, ctx, c_ctx, w_mod, b_mod, norm_gain, attn_w_in, attn_w_out, attn_qk_gain, attn_sink,
           rec_w_in, rec_w_out, rec_out_gain, rec_lb_gamma, ffn_w_up, ffn_conv_w, ffn_conv_b, ffn_w_down):
    bsz, n_lat, d = x.shape
    depth = w_mod.shape[0]
    assert depth == 2 and bsz < SUBLANES, "layer 0 attention, layer 1 recurrence; conditioning rows fit one tile"
    n_heads_rec = d // REC_DK

    cvec = jnp.zeros((SUBLANES, d), F32).at[:bsz].set(c).at[bsz].set(c_ctx)
    mods = _mods(cvec, w_mod, b_mod)

    def lat_mod(layer, k):
        return mods[layer, :bsz, k * d:(k + 1) * d][:, None, :]

    def ctx_mod(layer, k):
        return jnp.broadcast_to(mods[layer, bsz, k * d:(k + 1) * d][None, None, :], (bsz, 1, d))

    w_up = ffn_w_up.astype(BF16)
    w_down = ffn_w_down.astype(BF16)

    aq = N_Q_HEADS * HEAD_DIM
    akv = N_KV_HEADS * HEAD_DIM
    w_in = attn_w_in[0]
    qa, ka, va, qb, kb, vb = jnp.split(w_in, np.cumsum([aq, akv, akv, aq, akv])[:5].tolist(), axis=1)
    w_qk = jnp.concatenate([qa, qb, ka, kb], axis=1)
    w_v = jnp.concatenate([va, vb], axis=1)
    qkg = attn_qk_gain[0]
    q_scale = ATTN_SCALE * LOG2E
    head_gain = jnp.concatenate([jnp.tile(qkg[0] * q_scale, N_Q_HEADS), jnp.tile(qkg[2] * q_scale, N_Q_HEADS),
                                 jnp.tile(qkg[1], N_KV_HEADS), jnp.tile(qkg[3], N_KV_HEADS)])[None, :]
    cos, sin = _rope_tables(n_lat)
    g0 = norm_gain[0, 0]
    qk_l, v_l = _proj_qkv(x, g0, lat_mod(0, 0), lat_mod(0, 1), w_qk, w_v, head_gain, cos, sin)
    qk_c, v_c = _proj_qkv(ctx, g0, ctx_mod(0, 0), ctx_mod(0, 1), w_qk, w_v, head_gain)

    ka_cols = slice(2 * aq, 2 * aq + akv)
    k_all = jnp.concatenate([qk_c[:, :, ka_cols], qk_l[:, :, ka_cols]], axis=1)
    v_all = jnp.concatenate([v_c[:, :, :akv], v_l[:, :, :akv]], axis=1)
    ones = jnp.ones(v_all.shape[:2] + (HEAD_DIM,), BF16)
    v_ext = jnp.concatenate([v_all[:, :, :HEAD_DIM], ones, v_all[:, :, HEAD_DIM:], ones], axis=-1)
    sink = attn_sink[0].astype(F32)
    o_a = _attn_a(qk_l, k_all, v_ext)
    o_b = _attn_b(qk_l, v_l, qk_c, v_c, sink)
    o_c = _attn_ctx(qk_c, v_c, jnp.concatenate([jnp.full((N_Q_HEADS,), -jnp.inf, F32), sink]))

    w_out = attn_w_out[0].astype(BF16)
    x = _mm_res([o_a, o_b], [w_out[:aq], w_out[aq:]], x, lat_mod(0, 2))
    ctx = _mm_res([o_c], [w_out], ctx, ctx_mod(0, 2))
    g1 = norm_gain[0, 1]
    x = _conv_ffn(x, g1, lat_mod(0, 3), lat_mod(0, 4), lat_mod(0, 5), w_up[0], ffn_conv_w[0], ffn_conv_b[0], w_down[0])
    ctx = _conv_ffn(ctx, g1, ctx_mod(0, 3), ctx_mod(0, 4), ctx_mod(0, 5), w_up[0], ffn_conv_w[0], ffn_conv_b[0],
                    w_down[0])

    lb_soft = jax.nn.softmax(rec_lb_gamma.astype(F32), axis=1)
    lower = (jnp.cumsum(lb_soft, axis=1) - lb_soft[:, :1])[:, 1]
    rk = n_heads_rec * REC_DK
    w_q, w_ff, w_fb, w_i, w_g = jnp.split(rec_w_in[0], [rk, 2 * rk, 3 * rk, 3 * rk + d], axis=1)
    w_qig = jnp.concatenate([w_q, w_i, w_g], axis=1)
    w_f = jnp.concatenate([w_ff, w_fb], axis=1)
    lb = lower.reshape(1, 2 * rk)
    g0 = norm_gain[1, 0]
    lf_l, kf_l, qig_l = _proj_rec(x, g0, lat_mod(1, 0), lat_mod(1, 1), w_f, w_qig, lb)
    lf_c, kf_c, qig_c = _proj_rec(ctx, g0, ctx_mod(1, 0), ctx_mod(1, 1), w_f, w_qig, lb)

    s_zero = jnp.zeros((bsz, n_heads_rec, REC_DK, REC_DK), F32)
    s_fwd = _scan_state(kf_c, lf_c, qig_c, s_zero, 0)
    s_bwd = _scan_state(kf_c, lf_c, qig_c, s_zero, 1)

    def scans(base):
        def run(kf, lf, qig, s_f, s_b, gain):
            o_f = _scan_out(kf, lf, qig, s_f, 0, base)
            return _scan_out(kf, lf, qig, s_b, 1, base, other=o_f, out_gain=gain)
        return run

    worst_exp2 = FAST_BASE * jnp.max(-jnp.log2(lower))
    y = lax.cond(worst_exp2 <= MAX_EXP2, scans(FAST_BASE), scans(1),
                 kf_l, lf_l, qig_l, s_fwd, s_bwd, rec_out_gain[0])
    x = _mm_res([y], [rec_w_out[0].astype(BF16)], x, lat_mod(1, 2))
    g1 = norm_gain[1, 1]
    x = _conv_ffn(x, g1, lat_mod(1, 3), lat_mod(1, 4), lat_mod(1, 5), w_up[1], ffn_conv_w[1], ffn_conv_b[1], w_down[1])
    return x
```

```python
import functools
import math

import numpy as np
import jax
import jax.numpy as jnp
from jax import lax
from jax.experimental import pallas as pl
from jax.experimental.pallas import tpu as pltpu

F32 = jnp.float32
BF16 = jnp.bfloat16

HEAD_DIM = 128
GRID_W = 64
N_Q_HEADS = 8
N_KV_HEADS = 2
GROUP = N_Q_HEADS // N_KV_HEADS
WINDOW = 128
ROPE_THETA = 10000.0
ATTN_SCALE = HEAD_DIM ** -0.5
NEG_INF = -1e30
LOG2E = 1.4426950408889634
REC_DK = 128
FAST_BASE = 64
MAX_EXP2 = 100.0
EPS = 1e-6

LANES = 128
SUBLANES = 8
BF16_ROWS = 16
MXU_COLS = 256
CHAINS_PER_STAGE = 4
VMEM_LIMIT = 56 * 1024 * 1024
W_TILE_BYTES = 6 * 1024 * 1024


def _pick(n, pref, align):
    if n <= pref:
        return n
    t = (pref // align) * align
    while t >= align:
        if n % t == 0:
            return t
        t -= align
    raise ValueError(f"no tile for {n} (pref {pref}, align {align})")


def _params(sem):
    return pltpu.CompilerParams(dimension_semantics=sem, vmem_limit_bytes=VMEM_LIMIT)


def _dot(a, b):
    return jnp.dot(a, b, preferred_element_type=F32)


def _dot_nt(a, b):
    return lax.dot_general(a, b, (((1,), (1,)), ((), ())), preferred_element_type=F32)


def _mod_kernel(c_ref, w_ref, b_ref, o_ref):
    cv = c_ref[...]
    o_ref[...] = _dot(cv * jax.nn.sigmoid(cv), w_ref[...]) + b_ref[...]


def _mods(cvec, w_mod, b_mod):
    depth, d, n = w_mod.shape
    tn = _pick(n, 1024, LANES)
    return pl.pallas_call(
        _mod_kernel,
        grid=(depth, n // tn),
        in_specs=[pl.BlockSpec((SUBLANES, d), lambda l, j: (0, 0)),
                  pl.BlockSpec((None, d, tn), lambda l, j: (l, 0, j)),
                  pl.BlockSpec((None, 1, tn), lambda l, j: (l, 0, j))],
        out_specs=pl.BlockSpec((None, SUBLANES, tn), lambda l, j: (l, 0, j)),
        out_shape=jax.ShapeDtypeStruct((depth, SUBLANES, n), F32),
        compiler_params=_params(("arbitrary", "arbitrary")),
        name="adaln_mods",
    )(cvec, w_mod, b_mod.reshape(depth, 1, n))


def _norm_mod(x, g, sh, sc):
    ms = jnp.mean(x * x, axis=-1, keepdims=True)
    return (x * lax.rsqrt(ms + EPS)) * (g * (1.0 + sc)) + sh


def _fill_h(x_ref, g_ref, sh_ref, sc_ref, h_ref):
    @pl.when(pl.program_id(2) == 0)
    def _():
        h_ref[...] = _norm_mod(x_ref[...], g_ref[...], sh_ref[...], sc_ref[...]).astype(BF16)


def _plain_blocks(h, w_ref, start, width):
    return [_dot(h, w_ref[:, start + c0:start + min(c0 + MXU_COLS, width)]) for c0 in range(0, width, MXU_COLS)]


def _store_blocks(o_ref, blocks):
    c0 = 0
    for blk in blocks:
        o_ref[:, c0:c0 + blk.shape[1]] = blk.astype(o_ref.dtype)
        c0 += blk.shape[1]


def _proj_qkv_kernel(*refs, rope):
    if rope:
        x_ref, g_ref, sh_ref, sc_ref, w_ref, hg_ref, cos_ref, sin_ref, qk_ref, v_ref, h_ref = refs
    else:
        x_ref, g_ref, sh_ref, sc_ref, w_ref, hg_ref, qk_ref, v_ref, h_ref = refs
    _fill_h(x_ref, g_ref, sh_ref, sc_ref, h_ref)
    h = h_ref[...]
    n_qk = qk_ref.shape[1]
    accs = _plain_blocks(h, w_ref, 0, n_qk)
    v_blocks = _plain_blocks(h, w_ref, n_qk, v_ref.shape[1])
    heads = [acc[:, c:c + HEAD_DIM] for acc in accs for c in range(0, acc.shape[1], HEAD_DIM)]
    cols = [slice(c, c + HEAD_DIM) for c in range(0, n_qk, HEAD_DIM)]
    inv = [lax.rsqrt(jnp.mean(x * x, axis=-1, keepdims=True) + EPS) for x in heads]
    ys = [x * r * hg_ref[:, c] for x, r, c in zip(heads, inv, cols)]
    if rope:
        ys = [y * cos_ref[...] + pltpu.roll(y, HEAD_DIM // 2, 1) * sin_ref[...] for y in ys]
    for y, c in zip(ys, cols):
        qk_ref[:, c] = y.astype(qk_ref.dtype)
    _store_blocks(v_ref, v_blocks)


def _proj_rec_kernel(x_ref, g_ref, sh_ref, sc_ref, w_ref, lb_ref, lf_ref, k_ref, qig_ref, h_ref):
    _fill_h(x_ref, g_ref, sh_ref, sc_ref, h_ref)
    h = h_ref[...]
    n_f = lf_ref.shape[1]
    sigs = [jax.nn.sigmoid(z) for z in _plain_blocks(h, w_ref, 0, n_f)]
    plain = _plain_blocks(h, w_ref, n_f, qig_ref.shape[1])
    c0 = 0
    for sig in sigs:
        c = slice(c0, c0 + sig.shape[1])
        lb = lb_ref[:, c]
        lf_ref[:, c] = jnp.log2(lb + (1.0 - lb) * sig)
        k_ref[:, c] = ((1.0 - lb) * (1.0 - sig)).astype(k_ref.dtype)
        c0 += sig.shape[1]
    _store_blocks(qig_ref, plain)


def _interleave(w_a, w_b, n_step):
    ta, tb = w_a.shape[1] // n_step, w_b.shape[1] // n_step
    tiles = []
    for j in range(n_step):
        tiles += [w_a[:, j * ta:(j + 1) * ta], w_b[:, j * tb:(j + 1) * tb]]
    return jnp.concatenate(tiles, axis=1)


def _proj_call(kernel, name, x, gain, shift, scale, w, n_step, extras, extra_specs, outs, tm_pref=1024):
    bsz, t, d = x.shape
    tm = _pick(t, tm_pref, BF16_ROWS)
    in_specs = [pl.BlockSpec((None, tm, d), lambda b, i, j: (b, i, 0)),
                pl.BlockSpec((1, d), lambda b, i, j: (0, 0)),
                pl.BlockSpec((None, 1, d), lambda b, i, j: (b, 0, 0)),
                pl.BlockSpec((None, 1, d), lambda b, i, j: (b, 0, 0)),
                pl.BlockSpec((d, w.shape[1] // n_step), lambda b, i, j: (0, j))] + [s(tm) for s in extra_specs]
    return pl.pallas_call(
        kernel,
        grid=(bsz, t // tm, n_step),
        in_specs=in_specs,
        out_specs=[pl.BlockSpec((None, tm, n // n_step), lambda b, i, j: (b, i, j)) for n, _ in outs],
        out_shape=[jax.ShapeDtypeStruct((bsz, t, n), dt) for n, dt in outs],
        scratch_shapes=[pltpu.VMEM((tm, d), BF16)],
        compiler_params=_params(("arbitrary", "arbitrary", "arbitrary")),
        name=name,
    )(x, gain.reshape(1, d), shift, scale, w, *extras)


def _col_spec(width):
    return lambda tm: pl.BlockSpec((1, width), lambda b, i, j: (0, j))


def _pos_spec(tm):
    return pl.BlockSpec((tm, HEAD_DIM), lambda b, i, j: (i, 0))


def _n_steps(n_a, n_b):
    return math.gcd(n_a // MXU_COLS, n_b // MXU_COLS)


def _proj_qkv(x, gain, shift, scale, w_qk, w_v, head_gain, cos=None, sin=None):
    n_qk, n_v = w_qk.shape[1], w_v.shape[1]
    n_step = _n_steps(n_qk, n_v)
    w = _interleave(w_qk, w_v, n_step).astype(BF16)
    outs = ((n_qk, BF16), (n_v, BF16))
    hg_spec = _col_spec(n_qk // n_step)
    if cos is None:
        return _proj_call(functools.partial(_proj_qkv_kernel, rope=False), "proj_qkv_ctx", x, gain, shift, scale, w,
                          n_step, (head_gain,), (hg_spec,), outs)
    return _proj_call(functools.partial(_proj_qkv_kernel, rope=True), "proj_qkv_rope", x, gain, shift, scale, w,
                      n_step, (head_gain, cos, sin), (hg_spec, _pos_spec, _pos_spec), outs)


def _proj_rec(x, gain, shift, scale, w_f, w_qig, lb):
    n_f, n_p = w_f.shape[1], w_qig.shape[1]
    n_step = _n_steps(n_f, n_p)
    outs = ((n_f, F32), (n_f, BF16), (n_p, BF16))
    w = _interleave(w_f, w_qig, n_step).astype(BF16)
    return _proj_call(_proj_rec_kernel, "proj_rec", x, gain, shift, scale, w, n_step,
                      (lb,), (_col_spec(n_f // n_step),), outs)


def _mm_res_kernel(a_ref, w_ref, r_ref, gt_ref, o_ref):
    o_ref[...] = r_ref[...] + gt_ref[...] * _dot(a_ref[...], w_ref[...])


def _mm2_res_kernel(a1_ref, a2_ref, w1_ref, w2_ref, r_ref, gt_ref, o_ref):
    acc = _dot(a1_ref[...], w1_ref[...]) + _dot(a2_ref[...], w2_ref[...])
    o_ref[...] = r_ref[...] + gt_ref[...] * acc


def _mm_res(acts, ws, res, gate, tm_pref=1024):
    bsz, t, n = res.shape
    tm = _pick(t, tm_pref, BF16_ROWS)
    k_total = sum(w.shape[0] for w in ws)
    tn = _pick(n, max(LANES, W_TILE_BYTES // (2 * k_total)), LANES)
    kernel = _mm_res_kernel if len(acts) == 1 else _mm2_res_kernel
    in_specs = ([pl.BlockSpec((None, tm, a.shape[2]), lambda b, i, j: (b, i, 0)) for a in acts]
                + [pl.BlockSpec((w.shape[0], tn), lambda b, i, j: (0, j)) for w in ws]
                + [pl.BlockSpec((None, tm, tn), lambda b, i, j: (b, i, j)),
                   pl.BlockSpec((None, 1, tn), lambda b, i, j: (b, 0, j))])
    return pl.pallas_call(
        kernel,
        grid=(bsz, t // tm, n // tn),
        in_specs=in_specs,
        out_specs=pl.BlockSpec((None, tm, tn), lambda b, i, j: (b, i, j)),
        out_shape=jax.ShapeDtypeStruct((bsz, t, n), F32),
        compiler_params=_params(("arbitrary", "arbitrary", "arbitrary")),
        name="mm_res" if len(acts) == 1 else "mm2_res",
    )(*acts, *ws, res, gate)


def _ffn_up_kernel(xp_ref, x_ref, xn_ref, g_ref, sh_ref, sc_ref, wg_ref, wv_ref, cwg_ref, cwv_ref,
                   cbg_ref, cbv_ref, o_ref, h_ref, *, tm, halo):
    i = pl.program_id(1)
    last = pl.num_programs(1) - 1

    @pl.when(pl.program_id(2) == 0)
    def _():
        g, sh, sc = g_ref[...], sh_ref[...], sc_ref[...]
        hp = jnp.where(i > 0, _norm_mod(xp_ref[...], g, sh, sc), 0.0)
        hn = jnp.where(i < last, _norm_mod(xn_ref[...], g, sh, sc), 0.0)
        h_ref[0:halo, :] = hp.astype(BF16)
        h_ref[halo:halo + tm, :] = _norm_mod(x_ref[...], g, sh, sc).astype(BF16)
        h_ref[halo + tm:, :] = hn.astype(BF16)

    h = h_ref[...]
    rows = tm + 2 * halo

    def branch(w_ref, cw_ref, cb_ref):
        u = _dot(h, w_ref[...])
        u_prev = pltpu.roll(u, 1, 0)[halo:halo + tm]
        u_next = pltpu.roll(u, rows - 1, 0)[halo:halo + tm]
        cw = cw_ref[...]
        return u_prev * cw[0:1] + u[halo:halo + tm] * cw[1:2] + u_next * cw[2:3] + cb_ref[...]

    yg = branch(wg_ref, cwg_ref, cbg_ref)
    yv = branch(wv_ref, cwv_ref, cbv_ref)
    o_ref[...] = (yg * jax.nn.sigmoid(yg) * yv).astype(o_ref.dtype)


def _ffn_up(x, gain, shift, scale, w_up, conv_w, conv_b, tm_pref=1024, tn_pref=512):
    bsz, t, d = x.shape
    d_ff = w_up.shape[1] // 2
    halo = BF16_ROWS
    tm = _pick(t, tm_pref, halo)
    tn = _pick(d_ff, tn_pref, LANES)
    nj = d_ff // tn
    r = tm // halo
    n_halo = t // halo
    kern = functools.partial(_ffn_up_kernel, tm=tm, halo=halo)
    conv_b = conv_b.reshape(1, 2 * d_ff)
    vec = lambda b, i, j: (b, 0, 0)
    return pl.pallas_call(
        kern,
        grid=(bsz, t // tm, nj),
        in_specs=[pl.BlockSpec((None, halo, d), lambda b, i, j: (b, jnp.maximum(i * r - 1, 0), 0)),
                  pl.BlockSpec((None, tm, d), lambda b, i, j: (b, i, 0)),
                  pl.BlockSpec((None, halo, d), lambda b, i, j: (b, jnp.minimum((i + 1) * r, n_halo - 1), 0)),
                  pl.BlockSpec((1, d), lambda b, i, j: (0, 0)),
                  pl.BlockSpec((None, 1, d), vec),
                  pl.BlockSpec((None, 1, d), vec),
                  pl.BlockSpec((d, tn), lambda b, i, j: (0, j)),
                  pl.BlockSpec((d, tn), lambda b, i, j: (0, j + nj)),
                  pl.BlockSpec((conv_w.shape[0], tn), lambda b, i, j: (0, j)),
                  pl.BlockSpec((conv_w.shape[0], tn), lambda b, i, j: (0, j + nj)),
                  pl.BlockSpec((1, tn), lambda b, i, j: (0, j)),
                  pl.BlockSpec((1, tn), lambda b, i, j: (0, j + nj))],
        out_specs=pl.BlockSpec((None, tm, tn), lambda b, i, j: (b, i, j)),
        out_shape=jax.ShapeDtypeStruct((bsz, t, d_ff), BF16),
        scratch_shapes=[pltpu.VMEM((tm + 2 * halo, d), BF16)],
        compiler_params=_params(("arbitrary", "arbitrary", "arbitrary")),
        name="ffn_up_conv",
    )(x, x, x, gain.reshape(1, d), shift, scale, w_up, w_up, conv_w, conv_w, conv_b, conv_b)


def _attn_a_kernel(q_ref, k_ref, v_ref, o_ref, m_ref, acc_ref, *, tq, rs):
    step = pl.program_id(3)

    @pl.when(step == 0)
    def _():
        m_ref[...] = jnp.full(m_ref.shape, -jnp.inf, F32)
        acc_ref[...] = jnp.zeros(acc_ref.shape, F32)

    k = k_ref[...]
    v = v_ref[...]
    n_chunk = k.shape[0] // LANES
    chains = [(g, r) for g in range(GROUP) for r in range(tq // rs)]
    for start in range(0, len(chains), CHAINS_PER_STAGE):
        group = chains[start:start + CHAINS_PER_STAGE]
        rows = [slice(g * tq + r * rs, g * tq + (r + 1) * rs) for g, r in group]
        scores = [_dot_nt(q_ref[r * rs:(r + 1) * rs, g * HEAD_DIM:(g + 1) * HEAD_DIM], k) for g, r in group]
        chunks = [[s[:, c * LANES:(c + 1) * LANES] for c in range(n_chunk)] for s in scores]
        m_prev = [m_ref[rw, :] for rw in rows]
        m_new = [jnp.maximum(mp, jnp.max(functools.reduce(jnp.maximum, ch), axis=-1, keepdims=True))
                 for mp, ch in zip(m_prev, chunks)]
        probs = [jnp.concatenate([jnp.exp2(c - mn) for c in ch], axis=1).astype(BF16) for mn, ch in zip(m_new, chunks)]
        pv = [_dot(p, v) for p in probs]
        for rw, mp, mn, upd in zip(rows, m_prev, m_new, pv):
            alpha = jnp.exp2(mp - mn)
            acc_ref[rw, :] = jnp.concatenate([alpha, alpha], axis=1) * acc_ref[rw, :] + upd
            m_ref[rw, :] = mn

    @pl.when(step == pl.num_programs(3) - 1)
    def _():
        for g in range(GROUP):
            acc = acc_ref[g * tq:(g + 1) * tq, :]
            o_ref[:, g * HEAD_DIM:(g + 1) * HEAD_DIM] = (acc[:, :HEAD_DIM] / acc[:, HEAD_DIM:]).astype(o_ref.dtype)


def _attn_a(qk, k_all, v_ext, tq_pref=1024, tk_pref=3328, rs_pref=256):
    bsz, t, _ = qk.shape
    s = k_all.shape[1]
    tq = _pick(t, tq_pref, BF16_ROWS)
    rs = _pick(tq, rs_pref, BF16_ROWS)
    tk = _pick(s, tk_pref, LANES)
    gw = GROUP * HEAD_DIM
    m = GROUP * tq
    return pl.pallas_call(
        functools.partial(_attn_a_kernel, tq=tq, rs=rs),
        grid=(bsz, N_KV_HEADS, t // tq, s // tk),
        in_specs=[pl.BlockSpec((None, tq, gw), lambda b, h, i, j: (b, i, h)),
                  pl.BlockSpec((None, tk, HEAD_DIM), lambda b, h, i, j: (b, j, h)),
                  pl.BlockSpec((None, tk, 2 * HEAD_DIM), lambda b, h, i, j: (b, j, h))],
        out_specs=pl.BlockSpec((None, tq, gw), lambda b, h, i, j: (b, i, h)),
        out_shape=jax.ShapeDtypeStruct((bsz, t, N_KV_HEADS * gw), BF16),
        scratch_shapes=[pltpu.VMEM((m, HEAD_DIM), F32), pltpu.VMEM((m, 2 * HEAD_DIM), F32)],
        compiler_params=_params(("arbitrary", "arbitrary", "arbitrary", "arbitrary")),
        name="attn_global",
    )(qk, k_all, v_ext)


def _softmax_out(s, sink, v):
    sink = sink * LOG2E
    m = jnp.maximum(jnp.max(s, axis=-1, keepdims=True), sink)
    p = jnp.exp2(s - m)
    den = jnp.sum(p, axis=-1, keepdims=True) + jnp.exp2(sink - m)
    return _dot(p.astype(BF16), v) / den


def _attn_b_kernel(sink_ref, bias_ref, q_ref, kc_ref, vc_ref, kp_ref, km_ref, kn_ref, vp_ref, vm_ref, vn_ref,
                   o_ref, kcat_ref, vcat_ref):
    kv = pl.program_id(1)
    for dst, parts in ((kcat_ref, (kc_ref, kp_ref, km_ref, kn_ref)), (vcat_ref, (vc_ref, vp_ref, vm_ref, vn_ref))):
        off = 0
        for part in parts:
            dst[off:off + part.shape[0], :] = part[...]
            off += part.shape[0]
    kcat = kcat_ref[...]
    vcat = vcat_ref[...]
    bias = bias_ref[...]
    heads = range(GROUP)
    cols = [slice(g * HEAD_DIM, (g + 1) * HEAD_DIM) for g in heads]
    scores = [_dot_nt(q_ref[:, cols[g]], kcat) + bias for g in heads]
    sinks = [sink_ref[kv * GROUP + g] * LOG2E for g in heads]
    tops = [jnp.maximum(jnp.max(scores[g], axis=-1, keepdims=True), sinks[g]) for g in heads]
    probs = [jnp.exp2(scores[g] - tops[g]) for g in heads]
    dens = [jnp.sum(probs[g], axis=-1, keepdims=True) + jnp.exp2(sinks[g] - tops[g]) for g in heads]
    outs = [_dot(probs[g].astype(BF16), vcat) for g in heads]
    for g in heads:
        o_ref[:, cols[g]] = (outs[g] / dens[g]).astype(o_ref.dtype)


def _attn_b(qk, v, qk_c, v_c, sink, tq_pref=256):
    bsz, t, _ = qk.shape
    n_ctx = qk_c.shape[1]
    tq = _pick(t, tq_pref, WINDOW)
    r = tq // WINDOW
    n_w = t // WINDOW
    gw = GROUP * HEAD_DIM
    kb, vb = 18, 2
    prev = lambda c: (lambda b, h, i: (b, jnp.maximum(i * r - 1, 0), c + h))
    main = lambda c: (lambda b, h, i: (b, i, c + h))
    nxt = lambda c: (lambda b, h, i: (b, jnp.minimum((i + 1) * r, n_w - 1), c + h))
    nk = n_ctx + tq + 2 * WINDOW
    n_i = t // tq
    row = np.arange(tq)[:, None]
    col = np.arange(nk)[None, :]
    band = (col < n_ctx) | (np.abs(col - n_ctx - WINDOW - row) <= WINDOW)
    before = (col >= n_ctx) & (col < n_ctx + WINDOW)
    after = col >= n_ctx + WINDOW + tq
    bias = np.stack([np.where(band & ~(before & bool(v & 1)) & ~(after & bool(v & 2)), 0.0, NEG_INF)
                     for v in range(4)]).astype(np.float32)
    variant = lambda i: jnp.where(i == 0, 1, 0) + jnp.where(i == n_i - 1, 2, 0)
    return pl.pallas_call(
        _attn_b_kernel,
        grid=(bsz, N_KV_HEADS, n_i),
        in_specs=[pl.BlockSpec(memory_space=pltpu.SMEM),
                  pl.BlockSpec((None, tq, nk), lambda b, h, i: (variant(i), 0, 0)),
                  pl.BlockSpec((None, tq, gw), lambda b, h, i: (b, i, 2 + h)),
                  pl.BlockSpec((None, n_ctx, HEAD_DIM), lambda b, h, i: (b, 0, kb + h)),
                  pl.BlockSpec((None, n_ctx, HEAD_DIM), lambda b, h, i: (b, 0, vb + h)),
                  pl.BlockSpec((None, WINDOW, HEAD_DIM), prev(kb)),
                  pl.BlockSpec((None, tq, HEAD_DIM), main(kb)),
                  pl.BlockSpec((None, WINDOW, HEAD_DIM), nxt(kb)),
                  pl.BlockSpec((None, WINDOW, HEAD_DIM), prev(vb)),
                  pl.BlockSpec((None, tq, HEAD_DIM), main(vb)),
                  pl.BlockSpec((None, WINDOW, HEAD_DIM), nxt(vb))],
        out_specs=pl.BlockSpec((None, tq, gw), lambda b, h, i: (b, i, h)),
        out_shape=jax.ShapeDtypeStruct((bsz, t, N_KV_HEADS * gw), BF16),
        scratch_shapes=[pltpu.VMEM((nk, HEAD_DIM), BF16), pltpu.VMEM((nk, HEAD_DIM), BF16)],
        compiler_params=_params(("arbitrary", "arbitrary", "arbitrary")),
        name="attn_window",
    )(sink, jnp.asarray(bias), qk, qk_c, v_c, qk, qk, qk, v, v, v)


def _attn_ctx_kernel(sink_ref, q_ref, k_ref, v_ref, o_ref):
    grp = pl.program_id(1)
    k = k_ref[...]
    v = v_ref[...]
    for g in range(GROUP):
        cols = slice(g * HEAD_DIM, (g + 1) * HEAD_DIM)
        s = _dot_nt(q_ref[:, cols], k)
        o_ref[:, cols] = _softmax_out(s, sink_ref[grp * GROUP + g], v).astype(o_ref.dtype)


def _attn_ctx(qk_c, v_c, sink_all):
    bsz, n_ctx, _ = qk_c.shape
    gw = GROUP * HEAD_DIM
    n_grp = 2 * N_KV_HEADS
    return pl.pallas_call(
        _attn_ctx_kernel,
        grid=(bsz, n_grp),
        in_specs=[pl.BlockSpec(memory_space=pltpu.SMEM),
                  pl.BlockSpec((None, n_ctx, gw), lambda b, h: (b, 0, h)),
                  pl.BlockSpec((None, n_ctx, HEAD_DIM), lambda b, h: (b, 0, 16 + h)),
                  pl.BlockSpec((None, n_ctx, HEAD_DIM), lambda b, h: (b, 0, h))],
        out_specs=pl.BlockSpec((None, n_ctx, gw), lambda b, h: (b, 0, h)),
        out_shape=jax.ShapeDtypeStruct((bsz, n_ctx, n_grp * gw), BF16),
        compiler_params=_params(("arbitrary", "arbitrary")),
        name="attn_ctx",
    )(sink_all, qk_c, qk_c, v_c)


def _scan_consts(chunk, reverse, base):
    idx = np.arange(chunk)
    t = idx[:, None]
    s = idx[None, :]
    tri = ((s >= t) if reverse else (s <= t)).astype(np.float32)
    blocks = [tri]
    masks = []
    h = chunk // 2
    while h >= base:
        same = (t // (2 * h)) == (s // (2 * h))
        t_hi = (t // h) % 2 == 1
        s_hi = (s // h) % 2 == 1
        masks.append(same & (~t_hi & s_hi if reverse else t_hi & ~s_hi))
        ref = (idx // (2 * h)) * (2 * h) + (h if reverse else h - 1)
        blocks.append(tri - tri[ref, :])
        h //= 2
    same = (t // base) == (s // base)
    masks.append(same & ((s >= t) if reverse else (s <= t)))
    if base > 1:
        blocks.append(tri * same)
    dist = np.concatenate(blocks, axis=0)
    return np.stack(masks).astype(np.float32), np.concatenate([dist, dist], axis=1)


def _block_prep(order, rows, dist, reverse, base, k_ref, lf_ref, v_ref, q_ref=None, masks=None):
    chunk = rows(order[0]).stop - rows(order[0]).start
    n_level = (chunk // base).bit_length() - 1
    dists = {}
    for pos in range(0, len(order), 2):
        pair = order[pos:pos + 2]
        cols = []
        for c in pair:
            lf = lf_ref[rows(c), :]
            hi = lf.astype(BF16)
            cols.append(jnp.concatenate([hi, (lf - hi.astype(F32)).astype(BF16)], axis=0))
        dd = _dot(dist, cols[0] if len(cols) == 1 else jnp.concatenate(cols, axis=1))
        for i, c in enumerate(pair):
            dists[c] = dd[:, i * REC_DK:(i + 1) * REC_DK]
    preps = {}
    for c in order:
        d = dists[c]
        cum = d[0:chunk]
        total = cum[0:1, :] if reverse else cum[chunk - 1:chunk, :]
        k = k_ref[rows(c), :]
        prep = dict(decay=jnp.exp2(total), k_out=k * jnp.exp2(total - cum).astype(BF16))
        if q_ref is not None:
            q = q_ref[rows(c), :]
            prep["q_in"] = q * jnp.exp2(cum).astype(BF16)
            pairs = []
            for level in range(n_level):
                e = jnp.exp2(-jnp.abs(d[(level + 1) * chunk:(level + 2) * chunk])).astype(BF16)
                pairs.append((q * e, k * e))
            if base == 1:
                pairs.append((q, k))
            else:
                since = d[(n_level + 1) * chunk:(n_level + 2) * chunk]
                pairs.append((q * jnp.exp2(since).astype(BF16), k * jnp.exp2(-since).astype(BF16)))
            prep["pairs"] = pairs
        preps[c] = prep
    for c in order:
        prep = preps[c]
        prep["kv"] = _dot(v_ref[rows(c), :].astype(F32).T.astype(BF16), prep.pop("k_out"))
        if q_ref is not None:
            prep["terms"] = [_dot_nt(qe, ke) for qe, ke in prep.pop("pairs")]
    if q_ref is not None:
        for c in order:
            terms = preps[c].pop("terms")
            scores = jnp.where(masks[n_level] > 0.0, terms[n_level], 0.0)
            for level in range(n_level):
                scores = scores + masks[level] * terms[level]
            preps[c]["scores"] = scores.astype(BF16)
        for c in order:
            preps[c]["intra"] = _dot(preps[c].pop("scores"), v_ref[rows(c), :])
    return preps


def _chunk_apply(st, prep):
    st_new = st * prep["decay"] + prep["kv"]
    if "intra" not in prep:
        return st_new, None
    return st_new, _dot_nt(prep["q_in"], st.astype(BF16)) + prep["intra"]


def _scan_order(n, reverse):
    return list(range(n - 1, -1, -1) if reverse else range(n))


def _scan_state_kernel(k_ref, lf_ref, v_ref, s0_ref, dist_ref, sfin_ref, st_ref, *, chunk, reverse):
    step = pl.program_id(2)

    @pl.when(step == 0)
    def _():
        st_ref[...] = s0_ref[...]

    rows = lambda c: slice(c * chunk, (c + 1) * chunk)
    order = _scan_order(k_ref.shape[0] // chunk, reverse)
    preps = _block_prep(order, rows, dist_ref[...], reverse, 1, k_ref, lf_ref, v_ref)
    st = st_ref[...]
    for c in order:
        st, _ = _chunk_apply(st, preps[c])
    st_ref[...] = st

    @pl.when(step == pl.num_programs(2) - 1)
    def _():
        sfin_ref[...] = st


def _scan_out_kernel(*refs, chunk, reverse, final, base):
    if final:
        q_ref, k_ref, lf_ref, v_ref, s0_ref, dist_ref, masks_ref, of_ref, g_ref, gain_ref, o_ref, st_ref = refs
    else:
        q_ref, k_ref, lf_ref, v_ref, s0_ref, dist_ref, masks_ref, o_ref, st_ref = refs

    @pl.when(pl.program_id(2) == 0)
    def _():
        st_ref[...] = s0_ref[...]

    rows = lambda c: slice(c * chunk, (c + 1) * chunk)
    order = _scan_order(k_ref.shape[0] // chunk, reverse)
    preps = _block_prep(order, rows, dist_ref[...], reverse, base, k_ref, lf_ref, v_ref, q_ref, masks_ref[...])
    st = st_ref[...]
    for c in order:
        st, out = _chunk_apply(st, preps[c])
        if final:
            tot = out + of_ref[rows(c), :]
            ms = jnp.mean(tot * tot, axis=-1, keepdims=True)
            gate = g_ref[rows(c), :].astype(F32)
            out = tot * lax.rsqrt(ms + EPS) * gain_ref[...] * (gate * jax.nn.sigmoid(gate))
        o_ref[rows(c), :] = out.astype(o_ref.dtype)
    st_ref[...] = st


def _scan_geometry(t, blk_pref=4096, chunk_pref=128):
    chunk = _pick(t, chunk_pref, LANES)
    blk = _pick(t, blk_pref, chunk)
    return chunk, blk


def _scan_state(kf, lf, qig, s0, direction):
    bsz, t, _ = kf.shape
    n_h = s0.shape[1]
    reverse = direction == 1
    chunk, blk = _scan_geometry(t)
    n_blk = t // blk
    dist = _scan_consts(chunk, reverse, 1)[1][:chunk]
    pos = (lambda c: n_blk - 1 - c) if reverse else (lambda c: c)
    kcol = direction * n_h
    return pl.pallas_call(
        functools.partial(_scan_state_kernel, chunk=chunk, reverse=reverse),
        grid=(bsz, n_h, n_blk),
        in_specs=[pl.BlockSpec((None, blk, REC_DK), lambda b, h, c: (b, pos(c), kcol + h)),
                  pl.BlockSpec((None, blk, REC_DK), lambda b, h, c: (b, pos(c), kcol + h)),
                  pl.BlockSpec((None, blk, REC_DK), lambda b, h, c: (b, pos(c), n_h + h)),
                  pl.BlockSpec((None, None, REC_DK, REC_DK), lambda b, h, c: (b, h, 0, 0)),
                  pl.BlockSpec(dist.shape, lambda b, h, c: (0, 0))],
        out_specs=pl.BlockSpec((None, None, REC_DK, REC_DK), lambda b, h, c: (b, h, 0, 0)),
        out_shape=jax.ShapeDtypeStruct(s0.shape, F32),
        scratch_shapes=[pltpu.VMEM((REC_DK, REC_DK), F32)],
        compiler_params=_params(("arbitrary", "arbitrary", "arbitrary")),
        name="hgrn_state_bwd" if reverse else "hgrn_state_fwd",
    )(kf, lf, qig, s0, jnp.asarray(dist, BF16))


def _scan_out(kf, lf, qig, s0, direction, base, other=None, out_gain=None):
    bsz, t, _ = kf.shape
    n_h = s0.shape[1]
    reverse = direction == 1
    final = other is not None
    chunk, blk = _scan_geometry(t)
    n_blk = t // blk
    masks, dist = _scan_consts(chunk, reverse, base)
    pos = (lambda c: n_blk - 1 - c) if reverse else (lambda c: c)
    kcol = direction * n_h
    tok = lambda col: pl.BlockSpec((None, blk, REC_DK), lambda b, h, c: (b, pos(c), col + h))
    in_specs = [tok(0), tok(kcol), tok(kcol), tok(n_h),
                pl.BlockSpec((None, None, REC_DK, REC_DK), lambda b, h, c: (b, h, 0, 0)),
                pl.BlockSpec(dist.shape, lambda b, h, c: (0, 0)),
                pl.BlockSpec(masks.shape, lambda b, h, c: (0, 0, 0))]
    args = [qig, kf, lf, qig, s0, jnp.asarray(dist, BF16), jnp.asarray(masks)]
    if final:
        in_specs += [tok(0), tok(2 * n_h), pl.BlockSpec((1, REC_DK), lambda b, h, c: (0, 0))]
        args += [other, qig, out_gain.reshape(1, REC_DK)]
    return pl.pallas_call(
        functools.partial(_scan_out_kernel, chunk=chunk, reverse=reverse, final=final, base=base),
        grid=(bsz, n_h, n_blk),
        in_specs=in_specs,
        out_specs=tok(0),
        out_shape=jax.ShapeDtypeStruct((bsz, t, n_h * REC_DK), BF16 if final else F32),
        scratch_shapes=[pltpu.VMEM((REC_DK, REC_DK), F32)],
        compiler_params=_params(("arbitrary", "arbitrary", "arbitrary")),
        name="hgrn_scan_bwd" if reverse else "hgrn_scan_fwd",
    )(*args)


def _rope_tables(n_tokens):
    n_rows = n_tokens // GRID_W
    quarter = HEAD_DIM // 4
    inv_freq = ROPE_THETA ** (-jnp.arange(quarter, dtype=F32) / quarter)
    row_ang = jnp.arange(n_rows, dtype=F32)[:, None] * inv_freq
    col_ang = jnp.arange(GRID_W, dtype=F32)[:, None] * inv_freq

    def table(fn):
        by_row = jnp.broadcast_to(fn(row_ang)[:, None, :], (n_rows, GRID_W, quarter))
        by_col = jnp.broadcast_to(fn(col_ang)[None, :, :], (n_rows, GRID_W, quarter))
        return jnp.concatenate([by_row, by_col], axis=-1).reshape(n_tokens, 2 * quarter)

    cos, sin = table(jnp.cos), table(jnp.sin)
    return jnp.concatenate([cos, cos], axis=-1), jnp.concatenate([-sin, sin], axis=-1)


def _conv_ffn(x, gain, shift, scale, gate, w_up, conv_w, conv_b, w_down):
    act = _ffn_up(x, gain, shift, scale, w_up, conv_w, conv_b)
    return _mm_res([act], [w_down], x, gate)


def kernel(x, c, ctx, c_ctx, w_mod, b_mod, norm_gain, attn_w_in, attn_w_out, attn_qk_gain, attn_sink,
           rec_w_in, rec_w_out, rec_out_gain, rec_lb_gamma, ffn_w_up, ffn_conv_w, ffn_conv_b, ffn_w_down):
    bsz, n_lat, d = x.shape
    depth = w_mod.shape[0]
    assert depth == 2 and bsz < SUBLANES, "layer 0 attention, layer 1 recurrence; conditioning rows fit one tile"
    n_heads_rec = d // REC_DK

    cvec = jnp.zeros((SUBLANES, d), F32).at[:bsz].set(c).at[bsz].set(c_ctx)
    mods = _mods(cvec, w_mod, b_mod)

    def lat_mod(layer, k):
        return mods[layer, :bsz, k * d:(k + 1) * d][:, None, :]

    def ctx_mod(layer, k):
        return jnp.broadcast_to(mods[layer, bsz, k * d:(k + 1) * d][None, None, :], (bsz, 1, d))

    w_up = ffn_w_up.astype(BF16)
    w_down = ffn_w_down.astype(BF16)

    aq = N_Q_HEADS * HEAD_DIM
    akv = N_KV_HEADS * HEAD_DIM
    w_in = attn_w_in[0]
    qa, ka, va, qb, kb, vb = jnp.split(w_in, np.cumsum([aq, akv, akv, aq, akv])[:5].tolist(), axis=1)
    w_qk = jnp.concatenate([qa, qb, ka, kb], axis=1)
    w_v = jnp.concatenate([va, vb], axis=1)
    qkg = attn_qk_gain[0]
    q_scale = ATTN_SCALE * LOG2E
    head_gain = jnp.concatenate([jnp.tile(qkg[0] * q_scale, N_Q_HEADS), jnp.tile(qkg[2] * q_scale, N_Q_HEADS),
                                 jnp.tile(qkg[1], N_KV_HEADS), jnp.tile(qkg[3], N_KV_HEADS)])[None, :]
    cos, sin = _rope_tables(n_lat)
    g0 = norm_gain[0, 0]
    qk_l, v_l = _proj_qkv(x, g0, lat_mod(0, 0), lat_mod(0, 1), w_qk, w_v, head_gain, cos, sin)
    qk_c, v_c = _proj_qkv(ctx, g0, ctx_mod(0, 0), ctx_mod(0, 1), w_qk, w_v, head_gain)

    ka_cols = slice(2 * aq, 2 * aq + akv)
    k_all = jnp.concatenate([qk_c[:, :, ka_cols], qk_l[:, :, ka_cols]], axis=1)
    v_all = jnp.concatenate([v_c[:, :, :akv], v_l[:, :, :akv]], axis=1)
    ones = jnp.ones(v_all.shape[:2] + (HEAD_DIM,), BF16)
    v_ext = jnp.concatenate([v_all[:, :, :HEAD_DIM], ones, v_all[:, :, HEAD_DIM:], ones], axis=-1)
    sink = attn_sink[0].astype(F32)
    o_a = _attn_a(qk_l, k_all, v_ext)
    o_b = _attn_b(qk_l, v_l, qk_c, v_c, sink)
    o_c = _attn_ctx(qk_c, v_c, jnp.concatenate([jnp.full((N_Q_HEADS,), -jnp.inf, F32), sink]))

    w_out = attn_w_out[0].astype(BF16)
    x = _mm_res([o_a, o_b], [w_out[:aq], w_out[aq:]], x, lat_mod(0, 2))
    ctx = _mm_res([o_c], [w_out], ctx, ctx_mod(0, 2))
    g1 = norm_gain[0, 1]
    x = _conv_ffn(x, g1, lat_mod(0, 3), lat_mod(0, 4), lat_mod(0, 5), w_up[0], ffn_conv_w[0], ffn_conv_b[0], w_down[0])
    ctx = _conv_ffn(ctx, g1, ctx_mod(0, 3), ctx_mod(0, 4), ctx_mod(0, 5), w_up[0], ffn_conv_w[0], ffn_conv_b[0],
                    w_down[0])

    lb_soft = jax.nn.softmax(rec_lb_gamma.astype(F32), axis=1)
    lower = (jnp.cumsum(lb_soft, axis=1) - lb_soft[:, :1])[:, 1]
    rk = n_heads_rec * REC_DK
    w_q, w_ff, w_fb, w_i, w_g = jnp.split(rec_w_in[0], [rk, 2 * rk, 3 * rk, 3 * rk + d], axis=1)
    w_qig = jnp.concatenate([w_q, w_i, w_g], axis=1)
    w_f = jnp.concatenate([w_ff, w_fb], axis=1)
    lb = lower.reshape(1, 2 * rk)
    g0 = norm_gain[1, 0]
    lf_l, kf_l, qig_l = _proj_rec(x, g0, lat_mod(1, 0), lat_mod(1, 1), w_f, w_qig, lb)
    lf_c, kf_c, qig_c = _proj_rec(ctx, g0, ctx_mod(1, 0), ctx_mod(1, 1), w_f, w_qig, lb)

    s_zero = jnp.zeros((bsz, n_heads_rec, REC_DK, REC_DK), F32)
    s_fwd = _scan_state(kf_c, lf_c, qig_c, s_zero, 0)
    s_bwd = _scan_state(kf_c, lf_c, qig_c, s_zero, 1)

    def scans(base):
        def run(kf, lf, qig, s_f, s_b, gain):
            o_f = _scan_out(kf, lf, qig, s_f, 0, base)
            return _scan_out(kf, lf, qig, s_b, 1, base, other=o_f, out_gain=gain)
        return run

    worst_exp2 = FAST_BASE * jnp.max(-jnp.log2(lower))
    y = lax.cond(worst_exp2 <= MAX_EXP2, scans(FAST_BASE), scans(1),
                 kf_l, lf_l, qig_l, s_fwd, s_bwd, rec_out_gain[0])
    x = _mm_res([y], [rec_w_out[0].astype(BF16)], x, lat_mod(1, 2))
    g1 = norm_gain[1, 1]
    x = _conv_ffn(x, g1, lat_mod(1, 3), lat_mod(1, 4), lat_mod(1, 5), w_up[1], ffn_conv_w[1], ffn_conv_b[1], w_down[1])
    return x
```

```python
import functools
import math

import numpy as np
import jax
import jax.numpy as jnp
from jax import lax
from jax.experimental import pallas as pl
from jax.experimental.pallas import tpu as pltpu

F32 = jnp.float32
BF16 = jnp.bfloat16

HEAD_DIM = 128
GRID_W = 64
N_Q_HEADS = 8
N_KV_HEADS = 2
GROUP = N_Q_HEADS // N_KV_HEADS
WINDOW = 128
ROPE_THETA = 10000.0
ATTN_SCALE = HEAD_DIM ** -0.5
NEG_INF = -1e30
LOG2E = 1.4426950408889634
REC_DK = 128
FAST_BASE = 64
MAX_EXP2 = 100.0
EPS = 1e-6

LANES = 128
SUBLANES = 8
BF16_ROWS = 16
MXU_COLS = 256
CHAINS_PER_STAGE = 4
VMEM_LIMIT = 56 * 1024 * 1024
W_TILE_BYTES = 6 * 1024 * 1024
W_RESIDENT_BYTES = 8 * 1024 * 1024


def _pick(n, pref, align):
    if n <= pref:
        return n
    t = (pref // align) * align
    while t >= align:
        if n % t == 0:
            return t
        t -= align
    raise ValueError(f"no tile for {n} (pref {pref}, align {align})")


def _params(sem):
    return pltpu.CompilerParams(dimension_semantics=sem, vmem_limit_bytes=VMEM_LIMIT)


def _dot(a, b):
    return jnp.dot(a, b, preferred_element_type=F32)


def _dot_nt(a, b):
    return lax.dot_general(a, b, (((1,), (1,)), ((), ())), preferred_element_type=F32)


def _mod_kernel(c_ref, w_ref, b_ref, o_ref):
    cv = c_ref[...]
    o_ref[...] = _dot(cv * jax.nn.sigmoid(cv), w_ref[...]) + b_ref[...]


def _mods(cvec, w_mod, b_mod):
    depth, d, n = w_mod.shape
    tn = _pick(n, 1024, LANES)
    return pl.pallas_call(
        _mod_kernel,
        grid=(depth, n // tn),
        in_specs=[pl.BlockSpec((SUBLANES, d), lambda l, j: (0, 0)),
                  pl.BlockSpec((None, d, tn), lambda l, j: (l, 0, j)),
                  pl.BlockSpec((None, 1, tn), lambda l, j: (l, 0, j))],
        out_specs=pl.BlockSpec((None, SUBLANES, tn), lambda l, j: (l, 0, j)),
        out_shape=jax.ShapeDtypeStruct((depth, SUBLANES, n), F32),
        compiler_params=_params(("arbitrary", "arbitrary")),
        name="adaln_mods",
    )(cvec, w_mod, b_mod.reshape(depth, 1, n))


def _norm_mod(x, g, sh, sc):
    ms = jnp.mean(x * x, axis=-1, keepdims=True)
    return (x * lax.rsqrt(ms + EPS)) * (g * (1.0 + sc)) + sh


def _fill_h(x_ref, g_ref, sh_ref, sc_ref, h_ref):
    @pl.when(pl.program_id(2) == 0)
    def _():
        h_ref[...] = _norm_mod(x_ref[...], g_ref[...], sh_ref[...], sc_ref[...]).astype(BF16)


def _plain_blocks(h, w_ref, start, width):
    return [_dot(h, w_ref[:, start + c0:start + min(c0 + MXU_COLS, width)]) for c0 in range(0, width, MXU_COLS)]


def _store_blocks(o_ref, blocks):
    c0 = 0
    for blk in blocks:
        o_ref[:, c0:c0 + blk.shape[1]] = blk.astype(o_ref.dtype)
        c0 += blk.shape[1]


def _proj_qkv_kernel(*refs, rope):
    if rope:
        x_ref, g_ref, sh_ref, sc_ref, w_ref, hg_ref, cos_ref, sin_ref, qk_ref, v_ref, h_ref = refs
    else:
        x_ref, g_ref, sh_ref, sc_ref, w_ref, hg_ref, qk_ref, v_ref, h_ref = refs
    _fill_h(x_ref, g_ref, sh_ref, sc_ref, h_ref)
    h = h_ref[...]
    n_qk = qk_ref.shape[1]
    accs = _plain_blocks(h, w_ref, 0, n_qk)
    v_blocks = _plain_blocks(h, w_ref, n_qk, v_ref.shape[1])
    heads = [acc[:, c:c + HEAD_DIM] for acc in accs for c in range(0, acc.shape[1], HEAD_DIM)]
    cols = [slice(c, c + HEAD_DIM) for c in range(0, n_qk, HEAD_DIM)]
    inv = [lax.rsqrt(jnp.mean(x * x, axis=-1, keepdims=True) + EPS) for x in heads]
    ys = [x * r * hg_ref[:, c] for x, r, c in zip(heads, inv, cols)]
    if rope:
        ys = [y * cos_ref[...] + pltpu.roll(y, HEAD_DIM // 2, 1) * sin_ref[...] for y in ys]
    for y, c in zip(ys, cols):
        qk_ref[:, c] = y.astype(qk_ref.dtype)
    _store_blocks(v_ref, v_blocks)


def _proj_rec_kernel(x_ref, g_ref, sh_ref, sc_ref, w_ref, lb_ref, lf_ref, k_ref, qig_ref, h_ref):
    _fill_h(x_ref, g_ref, sh_ref, sc_ref, h_ref)
    h = h_ref[...]
    n_f = lf_ref.shape[1]
    sigs = [jax.nn.sigmoid(z) for z in _plain_blocks(h, w_ref, 0, n_f)]
    plain = _plain_blocks(h, w_ref, n_f, qig_ref.shape[1])
    c0 = 0
    for sig in sigs:
        c = slice(c0, c0 + sig.shape[1])
        lb = lb_ref[:, c]
        lf_ref[:, c] = jnp.log2(lb + (1.0 - lb) * sig)
        k_ref[:, c] = ((1.0 - lb) * (1.0 - sig)).astype(k_ref.dtype)
        c0 += sig.shape[1]
    _store_blocks(qig_ref, plain)


def _interleave(w_a, w_b, n_step):
    ta, tb = w_a.shape[1] // n_step, w_b.shape[1] // n_step
    tiles = []
    for j in range(n_step):
        tiles += [w_a[:, j * ta:(j + 1) * ta], w_b[:, j * tb:(j + 1) * tb]]
    return jnp.concatenate(tiles, axis=1)


def _proj_call(kernel, name, x, gain, shift, scale, w, n_step, extras, extra_specs, outs, tm_pref=1024):
    bsz, t, d = x.shape
    tm = _pick(t, tm_pref, BF16_ROWS)
    in_specs = [pl.BlockSpec((None, tm, d), lambda b, i, j: (b, i, 0)),
                pl.BlockSpec((1, d), lambda b, i, j: (0, 0)),
                pl.BlockSpec((None, 1, d), lambda b, i, j: (b, 0, 0)),
                pl.BlockSpec((None, 1, d), lambda b, i, j: (b, 0, 0)),
                pl.BlockSpec((d, w.shape[1] // n_step), lambda b, i, j: (0, j))] + [s(tm) for s in extra_specs]
    return pl.pallas_call(
        kernel,
        grid=(bsz, t // tm, n_step),
        in_specs=in_specs,
        out_specs=[pl.BlockSpec((None, tm, n // n_step), lambda b, i, j: (b, i, j)) for n, _ in outs],
        out_shape=[jax.ShapeDtypeStruct((bsz, t, n), dt) for n, dt in outs],
        scratch_shapes=[pltpu.VMEM((tm, d), BF16)],
        compiler_params=_params(("arbitrary", "arbitrary", "arbitrary")),
        name=name,
    )(x, gain.reshape(1, d), shift, scale, w, *extras)


def _col_spec(width):
    return lambda tm: pl.BlockSpec((1, width), lambda b, i, j: (0, j))


def _pos_spec(tm):
    return pl.BlockSpec((tm, HEAD_DIM), lambda b, i, j: (i, 0))


def _n_steps(n_a, n_b):
    return math.gcd(n_a // MXU_COLS, n_b // MXU_COLS)


def _proj_qkv(x, gain, shift, scale, w_qk, w_v, head_gain, cos=None, sin=None):
    n_qk, n_v = w_qk.shape[1], w_v.shape[1]
    n_step = _n_steps(n_qk, n_v)
    w = _interleave(w_qk, w_v, n_step).astype(BF16)
    outs = ((n_qk, BF16), (n_v, BF16))
    hg_spec = _col_spec(n_qk // n_step)
    if cos is None:
        return _proj_call(functools.partial(_proj_qkv_kernel, rope=False), "proj_qkv_ctx", x, gain, shift, scale, w,
                          n_step, (head_gain,), (hg_spec,), outs)
    return _proj_call(functools.partial(_proj_qkv_kernel, rope=True), "proj_qkv_rope", x, gain, shift, scale, w,
                      n_step, (head_gain, cos, sin), (hg_spec, _pos_spec, _pos_spec), outs)


def _proj_rec(x, gain, shift, scale, w_f, w_qig, lb):
    n_f, n_p = w_f.shape[1], w_qig.shape[1]
    n_step = _n_steps(n_f, n_p)
    outs = ((n_f, F32), (n_f, BF16), (n_p, BF16))
    w = _interleave(w_f, w_qig, n_step).astype(BF16)
    return _proj_call(_proj_rec_kernel, "proj_rec", x, gain, shift, scale, w, n_step,
                      (lb,), (_col_spec(n_f // n_step),), outs)


def _mm_res_kernel(a_ref, w_ref, r_ref, gt_ref, o_ref):
    o_ref[...] = r_ref[...] + gt_ref[...] * _dot(a_ref[...], w_ref[...])


def _mm2_res_kernel(a1_ref, a2_ref, w1_ref, w2_ref, r_ref, gt_ref, o_ref):
    acc = _dot(a1_ref[...], w1_ref[...]) + _dot(a2_ref[...], w2_ref[...])
    o_ref[...] = r_ref[...] + gt_ref[...] * acc


def _mm_res(acts, ws, res, gate, tm_pref=1024):
    bsz, t, n = res.shape
    k_total = sum(w.shape[0] for w in ws)
    if 2 * k_total * n <= W_RESIDENT_BYTES:
        tm, tn = _pick(t, tm_pref // 2, BF16_ROWS), n
    else:
        tm = _pick(t, tm_pref, BF16_ROWS)
        tn = _pick(n, max(LANES, W_TILE_BYTES // (2 * k_total)), LANES)
    kernel = _mm_res_kernel if len(acts) == 1 else _mm2_res_kernel
    in_specs = ([pl.BlockSpec((None, tm, a.shape[2]), lambda b, i, j: (b, i, 0)) for a in acts]
                + [pl.BlockSpec((w.shape[0], tn), lambda b, i, j: (0, j)) for w in ws]
                + [pl.BlockSpec((None, tm, tn), lambda b, i, j: (b, i, j)),
                   pl.BlockSpec((None, 1, tn), lambda b, i, j: (b, 0, j))])
    return pl.pallas_call(
        kernel,
        grid=(bsz, t // tm, n // tn),
        in_specs=in_specs,
        out_specs=pl.BlockSpec((None, tm, tn), lambda b, i, j: (b, i, j)),
        out_shape=jax.ShapeDtypeStruct((bsz, t, n), F32),
        compiler_params=_params(("arbitrary", "arbitrary", "arbitrary")),
        name="mm_res" if len(acts) == 1 else "mm2_res",
    )(*acts, *ws, res, gate)


def _ffn_up_kernel(xp_ref, x_ref, xn_ref, g_ref, sh_ref, sc_ref, wg_ref, wv_ref, cwg_ref, cwv_ref,
                   cbg_ref, cbv_ref, o_ref, h_ref, *, tm, halo):
    i = pl.program_id(1)
    last = pl.num_programs(1) - 1

    @pl.when(pl.program_id(2) == 0)
    def _():
        g, sh, sc = g_ref[...], sh_ref[...], sc_ref[...]
        hp = jnp.where(i > 0, _norm_mod(xp_ref[...], g, sh, sc), 0.0)
        hn = jnp.where(i < last, _norm_mod(xn_ref[...], g, sh, sc), 0.0)
        h_ref[0:halo, :] = hp.astype(BF16)
        h_ref[halo:halo + tm, :] = _norm_mod(x_ref[...], g, sh, sc).astype(BF16)
        h_ref[halo + tm:, :] = hn.astype(BF16)

    h = h_ref[...]
    rows = tm + 2 * halo

    def branch(w_ref, cw_ref, cb_ref):
        u = _dot(h, w_ref[...])
        u_prev = pltpu.roll(u, 1, 0)[halo:halo + tm]
        u_next = pltpu.roll(u, rows - 1, 0)[halo:halo + tm]
        cw = cw_ref[...]
        return u_prev * cw[0:1] + u[halo:halo + tm] * cw[1:2] + u_next * cw[2:3] + cb_ref[...]

    yg = branch(wg_ref, cwg_ref, cbg_ref)
    yv = branch(wv_ref, cwv_ref, cbv_ref)
    o_ref[...] = (yg * jax.nn.sigmoid(yg) * yv).astype(o_ref.dtype)


def _ffn_up(x, gain, shift, scale, w_up, conv_w, conv_b, tm_pref=1024, tn_pref=512):
    bsz, t, d = x.shape
    d_ff = w_up.shape[1] // 2
    halo = BF16_ROWS
    tm = _pick(t, tm_pref, halo)
    tn = _pick(d_ff, tn_pref, LANES)
    nj = d_ff // tn
    r = tm // halo
    n_halo = t // halo
    kern = functools.partial(_ffn_up_kernel, tm=tm, halo=halo)
    conv_b = conv_b.reshape(1, 2 * d_ff)
    vec = lambda b, i, j: (b, 0, 0)
    return pl.pallas_call(
        kern,
        grid=(bsz, t // tm, nj),
        in_specs=[pl.BlockSpec((None, halo, d), lambda b, i, j: (b, jnp.maximum(i * r - 1, 0), 0)),
                  pl.BlockSpec((None, tm, d), lambda b, i, j: (b, i, 0)),
                  pl.BlockSpec((None, halo, d), lambda b, i, j: (b, jnp.minimum((i + 1) * r, n_halo - 1), 0)),
                  pl.BlockSpec((1, d), lambda b, i, j: (0, 0)),
                  pl.BlockSpec((None, 1, d), vec),
                  pl.BlockSpec((None, 1, d), vec),
                  pl.BlockSpec((d, tn), lambda b, i, j: (0, j)),
                  pl.BlockSpec((d, tn), lambda b, i, j: (0, j + nj)),
                  pl.BlockSpec((conv_w.shape[0], tn), lambda b, i, j: (0, j)),
                  pl.BlockSpec((conv_w.shape[0], tn), lambda b, i, j: (0, j + nj)),
                  pl.BlockSpec((1, tn), lambda b, i, j: (0, j)),
                  pl.BlockSpec((1, tn), lambda b, i, j: (0, j + nj))],
        out_specs=pl.BlockSpec((None, tm, tn), lambda b, i, j: (b, i, j)),
        out_shape=jax.ShapeDtypeStruct((bsz, t, d_ff), BF16),
        scratch_shapes=[pltpu.VMEM((tm + 2 * halo, d), BF16)],
        compiler_params=_params(("arbitrary", "arbitrary", "arbitrary")),
        name="ffn_up_conv",
    )(x, x, x, gain.reshape(1, d), shift, scale, w_up, w_up, conv_w, conv_w, conv_b, conv_b)


def _attn_a_kernel(q_ref, k_ref, v_ref, o_ref, m_ref, acc_ref, *, tq, rs):
    step = pl.program_id(3)

    @pl.when(step == 0)
    def _():
        m_ref[...] = jnp.full(m_ref.shape, -jnp.inf, F32)
        acc_ref[...] = jnp.zeros(acc_ref.shape, F32)

    k = k_ref[...]
    v = v_ref[...]
    n_chunk = k.shape[0] // LANES
    chains = [(g, r) for g in range(GROUP) for r in range(tq // rs)]
    for start in range(0, len(chains), CHAINS_PER_STAGE):
        group = chains[start:start + CHAINS_PER_STAGE]
        rows = [slice(g * tq + r * rs, g * tq + (r + 1) * rs) for g, r in group]
        scores = [_dot_nt(q_ref[r * rs:(r + 1) * rs, g * HEAD_DIM:(g + 1) * HEAD_DIM], k) for g, r in group]
        chunks = [[s[:, c * LANES:(c + 1) * LANES] for c in range(n_chunk)] for s in scores]
        m_prev = [m_ref[rw, :] for rw in rows]
        m_new = [jnp.maximum(mp, jnp.max(functools.reduce(jnp.maximum, ch), axis=-1, keepdims=True))
                 for mp, ch in zip(m_prev, chunks)]
        probs = [jnp.concatenate([jnp.exp2(c - mn) for c in ch], axis=1).astype(BF16) for mn, ch in zip(m_new, chunks)]
        pv = [_dot(p, v) for p in probs]
        for rw, mp, mn, upd in zip(rows, m_prev, m_new, pv):
            alpha = jnp.exp2(mp - mn)
            acc_ref[rw, :] = jnp.concatenate([alpha, alpha], axis=1) * acc_ref[rw, :] + upd
            m_ref[rw, :] = mn

    @pl.when(step == pl.num_programs(3) - 1)
    def _():
        for g in range(GROUP):
            acc = acc_ref[g * tq:(g + 1) * tq, :]
            o_ref[:, g * HEAD_DIM:(g + 1) * HEAD_DIM] = (acc[:, :HEAD_DIM] / acc[:, HEAD_DIM:]).astype(o_ref.dtype)


def _attn_a(qk, k_all, v_ext, tq_pref=1024, tk_pref=3328, rs_pref=256):
    bsz, t, _ = qk.shape
    s = k_all.shape[1]
    tq = _pick(t, tq_pref, BF16_ROWS)
    rs = _pick(tq, rs_pref, BF16_ROWS)
    tk = _pick(s, tk_pref, LANES)
    gw = GROUP * HEAD_DIM
    m = GROUP * tq
    return pl.pallas_call(
        functools.partial(_attn_a_kernel, tq=tq, rs=rs),
        grid=(bsz, N_KV_HEADS, t // tq, s // tk),
        in_specs=[pl.BlockSpec((None, tq, gw), lambda b, h, i, j: (b, i, h)),
                  pl.BlockSpec((None, tk, HEAD_DIM), lambda b, h, i, j: (b, j, h)),
                  pl.BlockSpec((None, tk, 2 * HEAD_DIM), lambda b, h, i, j: (b, j, h))],
        out_specs=pl.BlockSpec((None, tq, gw), lambda b, h, i, j: (b, i, h)),
        out_shape=jax.ShapeDtypeStruct((bsz, t, N_KV_HEADS * gw), BF16),
        scratch_shapes=[pltpu.VMEM((m, HEAD_DIM), F32), pltpu.VMEM((m, 2 * HEAD_DIM), F32)],
        compiler_params=_params(("arbitrary", "arbitrary", "arbitrary", "arbitrary")),
        name="attn_global",
    )(qk, k_all, v_ext)


def _softmax_out(s, sink, v):
    sink = sink * LOG2E
    m = jnp.maximum(jnp.max(s, axis=-1, keepdims=True), sink)
    p = jnp.exp2(s - m)
    den = jnp.sum(p, axis=-1, keepdims=True) + jnp.exp2(sink - m)
    return _dot(p.astype(BF16), v) / den


def _attn_b_kernel(sink_ref, bias_ref, q_ref, kc_ref, vc_ref, kp_ref, km_ref, kn_ref, vp_ref, vm_ref, vn_ref,
                   o_ref, kcat_ref, vcat_ref):
    kv = pl.program_id(1)
    for dst, parts in ((kcat_ref, (kc_ref, kp_ref, km_ref, kn_ref)), (vcat_ref, (vc_ref, vp_ref, vm_ref, vn_ref))):
        off = 0
        for part in parts:
            dst[off:off + part.shape[0], :] = part[...]
            off += part.shape[0]
    kcat = kcat_ref[...]
    vcat = vcat_ref[...]
    bias = bias_ref[...]
    heads = range(GROUP)
    cols = [slice(g * HEAD_DIM, (g + 1) * HEAD_DIM) for g in heads]
    scores = [_dot_nt(q_ref[:, cols[g]], kcat) + bias for g in heads]
    sinks = [sink_ref[kv * GROUP + g] * LOG2E for g in heads]
    tops = [jnp.maximum(jnp.max(scores[g], axis=-1, keepdims=True), sinks[g]) for g in heads]
    probs = [jnp.exp2(scores[g] - tops[g]) for g in heads]
    dens = [jnp.sum(probs[g], axis=-1, keepdims=True) + jnp.exp2(sinks[g] - tops[g]) for g in heads]
    outs = [_dot(probs[g].astype(BF16), vcat) for g in heads]
    for g in heads:
        o_ref[:, cols[g]] = (outs[g] / dens[g]).astype(o_ref.dtype)


def _attn_b(qk, v, qk_c, v_c, sink, tq_pref=256):
    bsz, t, _ = qk.shape
    n_ctx = qk_c.shape[1]
    tq = _pick(t, tq_pref, WINDOW)
    r = tq // WINDOW
    n_w = t // WINDOW
    gw = GROUP * HEAD_DIM
    kb, vb = 18, 2
    prev = lambda c: (lambda b, h, i: (b, jnp.maximum(i * r - 1, 0), c + h))
    main = lambda c: (lambda b, h, i: (b, i, c + h))
    nxt = lambda c: (lambda b, h, i: (b, jnp.minimum((i + 1) * r, n_w - 1), c + h))
    nk = n_ctx + tq + 2 * WINDOW
    n_i = t // tq
    row = np.arange(tq)[:, None]
    col = np.arange(nk)[None, :]
    band = (col < n_ctx) | (np.abs(col - n_ctx - WINDOW - row) <= WINDOW)
    before = (col >= n_ctx) & (col < n_ctx + WINDOW)
    after = col >= n_ctx + WINDOW + tq
    bias = np.stack([np.where(band & ~(before & bool(v & 1)) & ~(after & bool(v & 2)), 0.0, NEG_INF)
                     for v in range(4)]).astype(np.float32)
    variant = lambda i: jnp.where(i == 0, 1, 0) + jnp.where(i == n_i - 1, 2, 0)
    return pl.pallas_call(
        _attn_b_kernel,
        grid=(bsz, N_KV_HEADS, n_i),
        in_specs=[pl.BlockSpec(memory_space=pltpu.SMEM),
                  pl.BlockSpec((None, tq, nk), lambda b, h, i: (variant(i), 0, 0)),
                  pl.BlockSpec((None, tq, gw), lambda b, h, i: (b, i, 2 + h)),
                  pl.BlockSpec((None, n_ctx, HEAD_DIM), lambda b, h, i: (b, 0, kb + h)),
                  pl.BlockSpec((None, n_ctx, HEAD_DIM), lambda b, h, i: (b, 0, vb + h)),
                  pl.BlockSpec((None, WINDOW, HEAD_DIM), prev(kb)),
                  pl.BlockSpec((None, tq, HEAD_DIM), main(kb)),
                  pl.BlockSpec((None, WINDOW, HEAD_DIM), nxt(kb)),
                  pl.BlockSpec((None, WINDOW, HEAD_DIM), prev(vb)),
                  pl.BlockSpec((None, tq, HEAD_DIM), main(vb)),
                  pl.BlockSpec((None, WINDOW, HEAD_DIM), nxt(vb))],
        out_specs=pl.BlockSpec((None, tq, gw), lambda b, h, i: (b, i, h)),
        out_shape=jax.ShapeDtypeStruct((bsz, t, N_KV_HEADS * gw), BF16),
        scratch_shapes=[pltpu.VMEM((nk, HEAD_DIM), BF16), pltpu.VMEM((nk, HEAD_DIM), BF16)],
        compiler_params=_params(("arbitrary", "arbitrary", "arbitrary")),
        name="attn_window",
    )(sink, jnp.asarray(bias), qk, qk_c, v_c, qk, qk, qk, v, v, v)


def _attn_ctx_kernel(sink_ref, q_ref, k_ref, v_ref, o_ref):
    grp = pl.program_id(1)
    k = k_ref[...]
    v = v_ref[...]
    for g in range(GROUP):
        cols = slice(g * HEAD_DIM, (g + 1) * HEAD_DIM)
        s = _dot_nt(q_ref[:, cols], k)
        o_ref[:, cols] = _softmax_out(s, sink_ref[grp * GROUP + g], v).astype(o_ref.dtype)


def _attn_ctx(qk_c, v_c, sink_all):
    bsz, n_ctx, _ = qk_c.shape
    gw = GROUP * HEAD_DIM
    n_grp = 2 * N_KV_HEADS
    return pl.pallas_call(
        _attn_ctx_kernel,
        grid=(bsz, n_grp),
        in_specs=[pl.BlockSpec(memory_space=pltpu.SMEM),
                  pl.BlockSpec((None, n_ctx, gw), lambda b, h: (b, 0, h)),
                  pl.BlockSpec((None, n_ctx, HEAD_DIM), lambda b, h: (b, 0, 16 + h)),
                  pl.BlockSpec((None, n_ctx, HEAD_DIM), lambda b, h: (b, 0, h))],
        out_specs=pl.BlockSpec((None, n_ctx, gw), lambda b, h: (b, 0, h)),
        out_shape=jax.ShapeDtypeStruct((bsz, n_ctx, n_grp * gw), BF16),
        compiler_params=_params(("arbitrary", "arbitrary")),
        name="attn_ctx",
    )(sink_all, qk_c, qk_c, v_c)


def _scan_consts(chunk, reverse, base):
    idx = np.arange(chunk)
    t = idx[:, None]
    s = idx[None, :]
    tri = ((s >= t) if reverse else (s <= t)).astype(np.float32)
    blocks = [tri]
    masks = []
    h = chunk // 2
    while h >= base:
        same = (t // (2 * h)) == (s // (2 * h))
        t_hi = (t // h) % 2 == 1
        s_hi = (s // h) % 2 == 1
        masks.append(same & (~t_hi & s_hi if reverse else t_hi & ~s_hi))
        ref = (idx // (2 * h)) * (2 * h) + (h if reverse else h - 1)
        blocks.append(tri - tri[ref, :])
        h //= 2
    same = (t // base) == (s // base)
    masks.append(same & ((s >= t) if reverse else (s <= t)))
    if base > 1:
        blocks.append(tri * same)
    dist = np.concatenate(blocks, axis=0)
    return np.stack(masks).astype(np.float32), np.concatenate([dist, dist], axis=1)


def _block_prep(order, rows, dist, reverse, base, k_ref, lf_ref, v_ref, q_ref=None, masks=None):
    chunk = rows(order[0]).stop - rows(order[0]).start
    n_level = (chunk // base).bit_length() - 1
    dists = {}
    for pos in range(0, len(order), 2):
        pair = order[pos:pos + 2]
        cols = []
        for c in pair:
            lf = lf_ref[rows(c), :]
            hi = lf.astype(BF16)
            cols.append(jnp.concatenate([hi, (lf - hi.astype(F32)).astype(BF16)], axis=0))
        dd = _dot(dist, cols[0] if len(cols) == 1 else jnp.concatenate(cols, axis=1))
        for i, c in enumerate(pair):
            dists[c] = dd[:, i * REC_DK:(i + 1) * REC_DK]
    preps = {}
    for c in order:
        d = dists[c]
        cum = d[0:chunk]
        total = cum[0:1, :] if reverse else cum[chunk - 1:chunk, :]
        k = k_ref[rows(c), :]
        prep = dict(decay=jnp.exp2(total), k_out=k * jnp.exp2(total - cum).astype(BF16))
        if q_ref is not None:
            q = q_ref[rows(c), :]
            prep["q_in"] = q * jnp.exp2(cum).astype(BF16)
            pairs = []
            for level in range(n_level):
                e = jnp.exp2(-jnp.abs(d[(level + 1) * chunk:(level + 2) * chunk])).astype(BF16)
                pairs.append((q * e, k * e))
            if base == 1:
                pairs.append((q, k))
            else:
                since = d[(n_level + 1) * chunk:(n_level + 2) * chunk]
                pairs.append((q * jnp.exp2(since).astype(BF16), k * jnp.exp2(-since).astype(BF16)))
            prep["pairs"] = pairs
        preps[c] = prep
    for c in order:
        prep = preps[c]
        prep["kv"] = _dot(v_ref[rows(c), :].astype(F32).T.astype(BF16), prep.pop("k_out"))
        if q_ref is not None:
            prep["terms"] = [_dot_nt(qe, ke) for qe, ke in prep.pop("pairs")]
    if q_ref is not None:
        for c in order:
            terms = preps[c].pop("terms")
            scores = jnp.where(masks[n_level] > 0.0, terms[n_level], 0.0)
            for level in range(n_level):
                scores = scores + masks[level] * terms[level]
            preps[c]["scores"] = scores.astype(BF16)
        for c in order:
            preps[c]["intra"] = _dot(preps[c].pop("scores"), v_ref[rows(c), :])
    return preps


def _chunk_apply(st, prep):
    st_new = st * prep["decay"] + prep["kv"]
    if "intra" not in prep:
        return st_new, None
    return st_new, _dot_nt(prep["q_in"], st.astype(BF16)) + prep["intra"]


def _scan_order(n, reverse):
    return list(range(n - 1, -1, -1) if reverse else range(n))


def _scan_state_kernel(k_ref, lf_ref, v_ref, s0_ref, dist_ref, sfin_ref, st_ref, *, chunk, reverse):
    step = pl.program_id(2)

    @pl.when(step == 0)
    def _():
        st_ref[...] = s0_ref[...]

    rows = lambda c: slice(c * chunk, (c + 1) * chunk)
    order = _scan_order(k_ref.shape[0] // chunk, reverse)
    preps = _block_prep(order, rows, dist_ref[...], reverse, 1, k_ref, lf_ref, v_ref)
    st = st_ref[...]
    for c in order:
        st, _ = _chunk_apply(st, preps[c])
    st_ref[...] = st

    @pl.when(step == pl.num_programs(2) - 1)
    def _():
        sfin_ref[...] = st


def _scan_out_kernel(*refs, chunk, reverse, final, base):
    if final:
        q_ref, k_ref, lf_ref, v_ref, s0_ref, dist_ref, masks_ref, of_ref, g_ref, gain_ref, o_ref, st_ref = refs
    else:
        q_ref, k_ref, lf_ref, v_ref, s0_ref, dist_ref, masks_ref, o_ref, st_ref = refs

    @pl.when(pl.program_id(2) == 0)
    def _():
        st_ref[...] = s0_ref[...]

    rows = lambda c: slice(c * chunk, (c + 1) * chunk)
    order = _scan_order(k_ref.shape[0] // chunk, reverse)
    preps = _block_prep(order, rows, dist_ref[...], reverse, base, k_ref, lf_ref, v_ref, q_ref, masks_ref[...])
    st = st_ref[...]
    for c in order:
        st, out = _chunk_apply(st, preps[c])
        if final:
            tot = out + of_ref[rows(c), :]
            ms = jnp.mean(tot * tot, axis=-1, keepdims=True)
            gate = g_ref[rows(c), :].astype(F32)
            out = tot * lax.rsqrt(ms + EPS) * gain_ref[...] * (gate * jax.nn.sigmoid(gate))
        o_ref[rows(c), :] = out.astype(o_ref.dtype)
    st_ref[...] = st


def _scan_geometry(t, blk_pref=4096, chunk_pref=128):
    chunk = _pick(t, chunk_pref, LANES)
    blk = _pick(t, blk_pref, chunk)
    return chunk, blk


def _scan_state(kf, lf, qig, s0, direction):
    bsz, t, _ = kf.shape
    n_h = s0.shape[1]
    reverse = direction == 1
    chunk, blk = _scan_geometry(t)
    n_blk = t // blk
    dist = _scan_consts(chunk, reverse, 1)[1][:chunk]
    pos = (lambda c: n_blk - 1 - c) if reverse else (lambda c: c)
    kcol = direction * n_h
    return pl.pallas_call(
        functools.partial(_scan_state_kernel, chunk=chunk, reverse=reverse),
        grid=(bsz, n_h, n_blk),
        in_specs=[pl.BlockSpec((None, blk, REC_DK), lambda b, h, c: (b, pos(c), kcol + h)),
                  pl.BlockSpec((None, blk, REC_DK), lambda b, h, c: (b, pos(c), kcol + h)),
                  pl.BlockSpec((None, blk, REC_DK), lambda b, h, c: (b, pos(c), n_h + h)),
                  pl.BlockSpec((None, None, REC_DK, REC_DK), lambda b, h, c: (b, h, 0, 0)),
                  pl.BlockSpec(dist.shape, lambda b, h, c: (0, 0))],
        out_specs=pl.BlockSpec((None, None, REC_DK, REC_DK), lambda b, h, c: (b, h, 0, 0)),
        out_shape=jax.ShapeDtypeStruct(s0.shape, F32),
        scratch_shapes=[pltpu.VMEM((REC_DK, REC_DK), F32)],
        compiler_params=_params(("arbitrary", "arbitrary", "arbitrary")),
        name="hgrn_state_bwd" if reverse else "hgrn_state_fwd",
    )(kf, lf, qig, s0, jnp.asarray(dist, BF16))


def _scan_out(kf, lf, qig, s0, direction, base, other=None, out_gain=None):
    bsz, t, _ = kf.shape
    n_h = s0.shape[1]
    reverse = direction == 1
    final = other is not None
    chunk, blk = _scan_geometry(t)
    n_blk = t // blk
    masks, dist = _scan_consts(chunk, reverse, base)
    pos = (lambda c: n_blk - 1 - c) if reverse else (lambda c: c)
    kcol = direction * n_h
    tok = lambda col: pl.BlockSpec((None, blk, REC_DK), lambda b, h, c: (b, pos(c), col + h))
    in_specs = [tok(0), tok(kcol), tok(kcol), tok(n_h),
                pl.BlockSpec((None, None, REC_DK, REC_DK), lambda b, h, c: (b, h, 0, 0)),
                pl.BlockSpec(dist.shape, lambda b, h, c: (0, 0)),
                pl.BlockSpec(masks.shape, lambda b, h, c: (0, 0, 0))]
    args = [qig, kf, lf, qig, s0, jnp.asarray(dist, BF16), jnp.asarray(masks)]
    if final:
        in_specs += [tok(0), tok(2 * n_h), pl.BlockSpec((1, REC_DK), lambda b, h, c: (0, 0))]
        args += [other, qig, out_gain.reshape(1, REC_DK)]
    return pl.pallas_call(
        functools.partial(_scan_out_kernel, chunk=chunk, reverse=reverse, final=final, base=base),
        grid=(bsz, n_h, n_blk),
        in_specs=in_specs,
        out_specs=tok(0),
        out_shape=jax.ShapeDtypeStruct((bsz, t, n_h * REC_DK), BF16 if final else F32),
        scratch_shapes=[pltpu.VMEM((REC_DK, REC_DK), F32)],
        compiler_params=_params(("arbitrary", "arbitrary", "arbitrary")),
        name="hgrn_scan_bwd" if reverse else "hgrn_scan_fwd",
    )(*args)


def _rope_tables(n_tokens):
    n_rows = n_tokens // GRID_W
    quarter = HEAD_DIM // 4
    inv_freq = ROPE_THETA ** (-jnp.arange(quarter, dtype=F32) / quarter)
    row_ang = jnp.arange(n_rows, dtype=F32)[:, None] * inv_freq
    col_ang = jnp.arange(GRID_W, dtype=F32)[:, None] * inv_freq

    def table(fn):
        by_row = jnp.broadcast_to(fn(row_ang)[:, None, :], (n_rows, GRID_W, quarter))
        by_col = jnp.broadcast_to(fn(col_ang)[None, :, :], (n_rows, GRID_W, quarter))
        return jnp.concatenate([by_row, by_col], axis=-1).reshape(n_tokens, 2 * quarter)

    cos, sin = table(jnp.cos), table(jnp.sin)
    return jnp.concatenate([cos, cos], axis=-1), jnp.concatenate([-sin, sin], axis=-1)


def _conv_ffn(x, gain, shift, scale, gate, w_up, conv_w, conv_b, w_down):
    act = _ffn_up(x, gain, shift, scale, w_up, conv_w, conv_b)
    return _mm_res([act], [w_down], x, gate)


def kernel(x, c, ctx, c_ctx, w_mod, b_mod, norm_gain, attn_w_in, attn_w_out, attn_qk_gain, attn_sink,
           rec_w_in, rec_w_out, rec_out_gain, rec_lb_gamma, ffn_w_up, ffn_conv_w, ffn_conv_b, ffn_w_down):
    bsz, n_lat, d = x.shape
    depth = w_mod.shape[0]
    assert depth == 2 and bsz < SUBLANES, "layer 0 attention, layer 1 recurrence; conditioning rows fit one tile"
    n_heads_rec = d // REC_DK

    cvec = jnp.zeros((SUBLANES, d), F32).at[:bsz].set(c).at[bsz].set(c_ctx)
    mods = _mods(cvec, w_mod, b_mod)

    def lat_mod(layer, k):
        return mods[layer, :bsz, k * d:(k + 1) * d][:, None, :]

    def ctx_mod(layer, k):
        return jnp.broadcast_to(mods[layer, bsz, k * d:(k + 1) * d][None, None, :], (bsz, 1, d))

    w_up = ffn_w_up.astype(BF16)
    w_down = ffn_w_down.astype(BF16)

    aq = N_Q_HEADS * HEAD_DIM
    akv = N_KV_HEADS * HEAD_DIM
    w_in = attn_w_in[0]
    qa, ka, va, qb, kb, vb = jnp.split(w_in, np.cumsum([aq, akv, akv, aq, akv])[:5].tolist(), axis=1)
    w_qk = jnp.concatenate([qa, qb, ka, kb], axis=1)
    w_v = jnp.concatenate([va, vb], axis=1)
    qkg = attn_qk_gain[0]
    q_scale = ATTN_SCALE * LOG2E
    head_gain = jnp.concatenate([jnp.tile(qkg[0] * q_scale, N_Q_HEADS), jnp.tile(qkg[2] * q_scale, N_Q_HEADS),
                                 jnp.tile(qkg[1], N_KV_HEADS), jnp.tile(qkg[3], N_KV_HEADS)])[None, :]
    cos, sin = _rope_tables(n_lat)
    g0 = norm_gain[0, 0]
    qk_l, v_l = _proj_qkv(x, g0, lat_mod(0, 0), lat_mod(0, 1), w_qk, w_v, head_gain, cos, sin)
    qk_c, v_c = _proj_qkv(ctx, g0, ctx_mod(0, 0), ctx_mod(0, 1), w_qk, w_v, head_gain)

    ka_cols = slice(2 * aq, 2 * aq + akv)
    k_all = jnp.concatenate([qk_c[:, :, ka_cols], qk_l[:, :, ka_cols]], axis=1)
    v_all = jnp.concatenate([v_c[:, :, :akv], v_l[:, :, :akv]], axis=1)
    ones = jnp.ones(v_all.shape[:2] + (HEAD_DIM,), BF16)
    v_ext = jnp.concatenate([v_all[:, :, :HEAD_DIM], ones, v_all[:, :, HEAD_DIM:], ones], axis=-1)
    sink = attn_sink[0].astype(F32)
    o_a = _attn_a(qk_l, k_all, v_ext)
    o_b = _attn_b(qk_l, v_l, qk_c, v_c, sink)
    o_c = _attn_ctx(qk_c, v_c, jnp.concatenate([jnp.full((N_Q_HEADS,), -jnp.inf, F32), sink]))

    w_out = attn_w_out[0].astype(BF16)
    x = _mm_res([o_a, o_b], [w_out[:aq], w_out[aq:]], x, lat_mod(0, 2))
    ctx = _mm_res([o_c], [w_out], ctx, ctx_mod(0, 2))
    g1 = norm_gain[0, 1]
    x = _conv_ffn(x, g1, lat_mod(0, 3), lat_mod(0, 4), lat_mod(0, 5), w_up[0], ffn_conv_w[0], ffn_conv_b[0], w_down[0])
    ctx = _conv_ffn(ctx, g1, ctx_mod(0, 3), ctx_mod(0, 4), ctx_mod(0, 5), w_up[0], ffn_conv_w[0], ffn_conv_b[0],
                    w_down[0])

    lb_soft = jax.nn.softmax(rec_lb_gamma.astype(F32), axis=1)
    lower = (jnp.cumsum(lb_soft, axis=1) - lb_soft[:, :1])[:, 1]
    rk = n_heads_rec * REC_DK
    w_q, w_ff, w_fb, w_i, w_g = jnp.split(rec_w_in[0], [rk, 2 * rk, 3 * rk, 3 * rk + d], axis=1)
    w_qig = jnp.concatenate([w_q, w_i, w_g], axis=1)
    w_f = jnp.concatenate([w_ff, w_fb], axis=1)
    lb = lower.reshape(1, 2 * rk)
    g0 = norm_gain[1, 0]
    lf_l, kf_l, qig_l = _proj_rec(x, g0, lat_mod(1, 0), lat_mod(1, 1), w_f, w_qig, lb)
    lf_c, kf_c, qig_c = _proj_rec(ctx, g0, ctx_mod(1, 0), ctx_mod(1, 1), w_f, w_qig, lb)

    s_zero = jnp.zeros((bsz, n_heads_rec, REC_DK, REC_DK), F32)
    s_fwd = _scan_state(kf_c, lf_c, qig_c, s_zero, 0)
    s_bwd = _scan_state(kf_c, lf_c, qig_c, s_zero, 1)

    def scans(base):
        def run(kf, lf, qig, s_f, s_b, gain):
            o_f = _scan_out(kf, lf, qig, s_f, 0, base)
            return _scan_out(kf, lf, qig, s_b, 1, base, other=o_f, out_gain=gain)
        return run

    worst_exp2 = FAST_BASE * jnp.max(-jnp.log2(lower))
    y = lax.cond(worst_exp2 <= MAX_EXP2, scans(FAST_BASE), scans(1),
                 kf_l, lf_l, qig_l, s_fwd, s_bwd, rec_out_gain[0])
    x = _mm_res([y], [rec_w_out[0].astype(BF16)], x, lat_mod(1, 2))
    g1 = norm_gain[1, 1]
    x = _conv_ffn(x, g1, lat_mod(1, 3), lat_mod(1, 4), lat_mod(1, 5), w_up[1], ffn_conv_w[1], ffn_conv_b[1], w_down[1])
    return x
```

```python
import functools
import math

import numpy as np
import jax
import jax.numpy as jnp
from jax import lax
from jax.experimental import pallas as pl
from jax.experimental.pallas import tpu as pltpu

F32 = jnp.float32
BF16 = jnp.bfloat16

HEAD_DIM = 128
GRID_W = 64
N_Q_HEADS = 8
N_KV_HEADS = 2
GROUP = N_Q_HEADS // N_KV_HEADS
WINDOW = 128
ROPE_THETA = 10000.0
ATTN_SCALE = HEAD_DIM ** -0.5
NEG_INF = -1e30
LOG2E = 1.4426950408889634
REC_DK = 128
FAST_BASE = 64
MAX_EXP2 = 100.0
EPS = 1e-6

LANES = 128
SUBLANES = 8
BF16_ROWS = 16
MXU_COLS = 256
CHAINS_PER_STAGE = 4
VMEM_LIMIT = 56 * 1024 * 1024
W_TILE_BYTES = 6 * 1024 * 1024
W_RESIDENT_BYTES = 8 * 1024 * 1024


def _pick(n, pref, align):
    if n <= pref:
        return n
    t = (pref // align) * align
    while t >= align:
        if n % t == 0:
            return t
        t -= align
    raise ValueError(f"no tile for {n} (pref {pref}, align {align})")


def _params(sem):
    return pltpu.CompilerParams(dimension_semantics=sem, vmem_limit_bytes=VMEM_LIMIT)


def _dot(a, b):
    return jnp.dot(a, b, preferred_element_type=F32)


def _dot_nt(a, b):
    return lax.dot_general(a, b, (((1,), (1,)), ((), ())), preferred_element_type=F32)


def _mod_kernel(c_ref, w_ref, b_ref, o_ref):
    cv = c_ref[...]
    o_ref[...] = _dot(cv * jax.nn.sigmoid(cv), w_ref[...]) + b_ref[...]


def _mods(cvec, w_mod, b_mod):
    depth, d, n = w_mod.shape
    tn = _pick(n, 1024, LANES)
    return pl.pallas_call(
        _mod_kernel,
        grid=(depth, n // tn),
        in_specs=[pl.BlockSpec((SUBLANES, d), lambda l, j: (0, 0)),
                  pl.BlockSpec((None, d, tn), lambda l, j: (l, 0, j)),
                  pl.BlockSpec((None, 1, tn), lambda l, j: (l, 0, j))],
        out_specs=pl.BlockSpec((None, SUBLANES, tn), lambda l, j: (l, 0, j)),
        out_shape=jax.ShapeDtypeStruct((depth, SUBLANES, n), F32),
        compiler_params=_params(("arbitrary", "arbitrary")),
        name="adaln_mods",
    )(cvec, w_mod, b_mod.reshape(depth, 1, n))


def _norm_mod(x, g, sh, sc):
    ms = jnp.mean(x * x, axis=-1, keepdims=True)
    return (x * lax.rsqrt(ms + EPS)) * (g * (1.0 + sc)) + sh


def _fill_h(x_ref, g_ref, sh_ref, sc_ref, h_ref):
    @pl.when(pl.program_id(2) == 0)
    def _():
        h_ref[...] = _norm_mod(x_ref[...], g_ref[...], sh_ref[...], sc_ref[...]).astype(BF16)


def _plain_blocks(h, w_ref, start, width):
    return [_dot(h, w_ref[:, start + c0:start + min(c0 + MXU_COLS, width)]) for c0 in range(0, width, MXU_COLS)]


def _store_blocks(o_ref, blocks):
    c0 = 0
    for blk in blocks:
        o_ref[:, c0:c0 + blk.shape[1]] = blk.astype(o_ref.dtype)
        c0 += blk.shape[1]


def _proj_qkv_kernel(*refs, rope):
    if rope:
        x_ref, g_ref, sh_ref, sc_ref, w_ref, hg_ref, cos_ref, sin_ref, qk_ref, v_ref, h_ref = refs
    else:
        x_ref, g_ref, sh_ref, sc_ref, w_ref, hg_ref, qk_ref, v_ref, h_ref = refs
    _fill_h(x_ref, g_ref, sh_ref, sc_ref, h_ref)
    h = h_ref[...]
    n_qk = qk_ref.shape[1]
    accs = _plain_blocks(h, w_ref, 0, n_qk)
    v_blocks = _plain_blocks(h, w_ref, n_qk, v_ref.shape[1])
    heads = [acc[:, c:c + HEAD_DIM] for acc in accs for c in range(0, acc.shape[1], HEAD_DIM)]
    cols = [slice(c, c + HEAD_DIM) for c in range(0, n_qk, HEAD_DIM)]
    inv = [lax.rsqrt(jnp.mean(x * x, axis=-1, keepdims=True) + EPS) for x in heads]
    ys = [x * r * hg_ref[:, c] for x, r, c in zip(heads, inv, cols)]
    if rope:
        ys = [y * cos_ref[...] + pltpu.roll(y, HEAD_DIM // 2, 1) * sin_ref[...] for y in ys]
    for y, c in zip(ys, cols):
        qk_ref[:, c] = y.astype(qk_ref.dtype)
    _store_blocks(v_ref, v_blocks)


def _proj_rec_kernel(x_ref, g_ref, sh_ref, sc_ref, w_ref, lb_ref, lf_ref, k_ref, qig_ref, h_ref):
    _fill_h(x_ref, g_ref, sh_ref, sc_ref, h_ref)
    h = h_ref[...]
    n_f = lf_ref.shape[1]
    sigs = [jax.nn.sigmoid(z) for z in _plain_blocks(h, w_ref, 0, n_f)]
    plain = _plain_blocks(h, w_ref, n_f, qig_ref.shape[1])
    c0 = 0
    for sig in sigs:
        c = slice(c0, c0 + sig.shape[1])
        lb = lb_ref[:, c]
        lf_ref[:, c] = jnp.log2(lb + (1.0 - lb) * sig)
        k_ref[:, c] = ((1.0 - lb) * (1.0 - sig)).astype(k_ref.dtype)
        c0 += sig.shape[1]
    _store_blocks(qig_ref, plain)


def _interleave(w_a, w_b, n_step):
    ta, tb = w_a.shape[1] // n_step, w_b.shape[1] // n_step
    tiles = []
    for j in range(n_step):
        tiles += [w_a[:, j * ta:(j + 1) * ta], w_b[:, j * tb:(j + 1) * tb]]
    return jnp.concatenate(tiles, axis=1)


def _proj_call(kernel, name, x, gain, shift, scale, w, n_step, extras, extra_specs, outs, tm_pref=1024):
    bsz, t, d = x.shape
    tm = _pick(t, tm_pref, BF16_ROWS)
    in_specs = [pl.BlockSpec((None, tm, d), lambda b, i, j: (b, i, 0)),
                pl.BlockSpec((1, d), lambda b, i, j: (0, 0)),
                pl.BlockSpec((None, 1, d), lambda b, i, j: (b, 0, 0)),
                pl.BlockSpec((None, 1, d), lambda b, i, j: (b, 0, 0)),
                pl.BlockSpec((d, w.shape[1] // n_step), lambda b, i, j: (0, j))] + [s(tm) for s in extra_specs]
    return pl.pallas_call(
        kernel,
        grid=(bsz, t // tm, n_step),
        in_specs=in_specs,
        out_specs=[pl.BlockSpec((None, tm, n // n_step), lambda b, i, j: (b, i, j)) for n, _ in outs],
        out_shape=[jax.ShapeDtypeStruct((bsz, t, n), dt) for n, dt in outs],
        scratch_shapes=[pltpu.VMEM((tm, d), BF16)],
        compiler_params=_params(("arbitrary", "arbitrary", "arbitrary")),
        name=name,
    )(x, gain.reshape(1, d), shift, scale, w, *extras)


def _col_spec(width):
    return lambda tm: pl.BlockSpec((1, width), lambda b, i, j: (0, j))


def _pos_spec(tm):
    return pl.BlockSpec((tm, HEAD_DIM), lambda b, i, j: (i, 0))


def _n_steps(n_a, n_b):
    return math.gcd(n_a // MXU_COLS, n_b // MXU_COLS)


def _proj_qkv(x, gain, shift, scale, w_qk, w_v, head_gain, cos=None, sin=None):
    n_qk, n_v = w_qk.shape[1], w_v.shape[1]
    n_step = _n_steps(n_qk, n_v)
    w = _interleave(w_qk, w_v, n_step).astype(BF16)
    outs = ((n_qk, BF16), (n_v, BF16))
    hg_spec = _col_spec(n_qk // n_step)
    if cos is None:
        return _proj_call(functools.partial(_proj_qkv_kernel, rope=False), "proj_qkv_ctx", x, gain, shift, scale, w,
                          n_step, (head_gain,), (hg_spec,), outs)
    return _proj_call(functools.partial(_proj_qkv_kernel, rope=True), "proj_qkv_rope", x, gain, shift, scale, w,
                      n_step, (head_gain, cos, sin), (hg_spec, _pos_spec, _pos_spec), outs)


def _proj_rec(x, gain, shift, scale, w_f, w_qig, lb):
    n_f, n_p = w_f.shape[1], w_qig.shape[1]
    n_step = _n_steps(n_f, n_p)
    outs = ((n_f, F32), (n_f, BF16), (n_p, BF16))
    w = _interleave(w_f, w_qig, n_step).astype(BF16)
    return _proj_call(_proj_rec_kernel, "proj_rec", x, gain, shift, scale, w, n_step,
                      (lb,), (_col_spec(n_f // n_step),), outs)


def _mm_res_kernel(a_ref, w_ref, r_ref, gt_ref, o_ref):
    o_ref[...] = r_ref[...] + gt_ref[...] * _dot(a_ref[...], w_ref[...])


def _mm2_res_kernel(a1_ref, a2_ref, w1_ref, w2_ref, r_ref, gt_ref, o_ref):
    acc = _dot(a1_ref[...], w1_ref[...]) + _dot(a2_ref[...], w2_ref[...])
    o_ref[...] = r_ref[...] + gt_ref[...] * acc


def _mm_res(acts, ws, res, gate, tm_pref=1024):
    bsz, t, n = res.shape
    k_total = sum(w.shape[0] for w in ws)
    w_mode = {}
    if 2 * k_total * n <= W_RESIDENT_BYTES:
        tm, tn = _pick(t, tm_pref // 2, BF16_ROWS), n
    elif 2 * k_total * n <= 3 * W_RESIDENT_BYTES:
        tm, tn = _pick(t, tm_pref // 2, BF16_ROWS), n
        w_mode = dict(pipeline_mode=pl.Buffered(1))
    else:
        tm = _pick(t, tm_pref, BF16_ROWS)
        tn = _pick(n, max(LANES, W_TILE_BYTES // (2 * k_total)), LANES)
    kernel = _mm_res_kernel if len(acts) == 1 else _mm2_res_kernel
    in_specs = ([pl.BlockSpec((None, tm, a.shape[2]), lambda b, i, j: (b, i, 0)) for a in acts]
                + [pl.BlockSpec((w.shape[0], tn), lambda b, i, j: (0, j), **w_mode) for w in ws]
                + [pl.BlockSpec((None, tm, tn), lambda b, i, j: (b, i, j)),
                   pl.BlockSpec((None, 1, tn), lambda b, i, j: (b, 0, j))])
    return pl.pallas_call(
        kernel,
        grid=(bsz, t // tm, n // tn),
        in_specs=in_specs,
        out_specs=pl.BlockSpec((None, tm, tn), lambda b, i, j: (b, i, j)),
        out_shape=jax.ShapeDtypeStruct((bsz, t, n), F32),
        compiler_params=_params(("arbitrary", "arbitrary", "arbitrary")),
        name="mm_res" if len(acts) == 1 else "mm2_res",
    )(*acts, *ws, res, gate)


def _ffn_up_kernel(xp_ref, x_ref, xn_ref, g_ref, sh_ref, sc_ref, wg_ref, wv_ref, cwg_ref, cwv_ref,
                   cbg_ref, cbv_ref, o_ref, h_ref, *, tm, halo):
    i = pl.program_id(1)
    last = pl.num_programs(1) - 1

    @pl.when(pl.program_id(2) == 0)
    def _():
        g, sh, sc = g_ref[...], sh_ref[...], sc_ref[...]
        hp = jnp.where(i > 0, _norm_mod(xp_ref[...], g, sh, sc), 0.0)
        hn = jnp.where(i < last, _norm_mod(xn_ref[...], g, sh, sc), 0.0)
        h_ref[0:halo, :] = hp.astype(BF16)
        h_ref[halo:halo + tm, :] = _norm_mod(x_ref[...], g, sh, sc).astype(BF16)
        h_ref[halo + tm:, :] = hn.astype(BF16)

    h = h_ref[...]
    rows = tm + 2 * halo

    def branch(w_ref, cw_ref, cb_ref):
        u = _dot(h, w_ref[...])
        u_prev = pltpu.roll(u, 1, 0)[halo:halo + tm]
        u_next = pltpu.roll(u, rows - 1, 0)[halo:halo + tm]
        cw = cw_ref[...]
        return u_prev * cw[0:1] + u[halo:halo + tm] * cw[1:2] + u_next * cw[2:3] + cb_ref[...]

    yg = branch(wg_ref, cwg_ref, cbg_ref)
    yv = branch(wv_ref, cwv_ref, cbv_ref)
    o_ref[...] = (yg * jax.nn.sigmoid(yg) * yv).astype(o_ref.dtype)


def _ffn_up(x, gain, shift, scale, w_up, conv_w, conv_b, tm_pref=1024, tn_pref=512):
    bsz, t, d = x.shape
    d_ff = w_up.shape[1] // 2
    halo = BF16_ROWS
    tm = _pick(t, tm_pref, halo)
    tn = _pick(d_ff, tn_pref, LANES)
    nj = d_ff // tn
    r = tm // halo
    n_halo = t // halo
    kern = functools.partial(_ffn_up_kernel, tm=tm, halo=halo)
    conv_b = conv_b.reshape(1, 2 * d_ff)
    vec = lambda b, i, j: (b, 0, 0)
    return pl.pallas_call(
        kern,
        grid=(bsz, t // tm, nj),
        in_specs=[pl.BlockSpec((None, halo, d), lambda b, i, j: (b, jnp.maximum(i * r - 1, 0), 0)),
                  pl.BlockSpec((None, tm, d), lambda b, i, j: (b, i, 0)),
                  pl.BlockSpec((None, halo, d), lambda b, i, j: (b, jnp.minimum((i + 1) * r, n_halo - 1), 0)),
                  pl.BlockSpec((1, d), lambda b, i, j: (0, 0)),
                  pl.BlockSpec((None, 1, d), vec),
                  pl.BlockSpec((None, 1, d), vec),
                  pl.BlockSpec((d, tn), lambda b, i, j: (0, j)),
                  pl.BlockSpec((d, tn), lambda b, i, j: (0, j + nj)),
                  pl.BlockSpec((conv_w.shape[0], tn), lambda b, i, j: (0, j)),
                  pl.BlockSpec((conv_w.shape[0], tn), lambda b, i, j: (0, j + nj)),
                  pl.BlockSpec((1, tn), lambda b, i, j: (0, j)),
                  pl.BlockSpec((1, tn), lambda b, i, j: (0, j + nj))],
        out_specs=pl.BlockSpec((None, tm, tn), lambda b, i, j: (b, i, j)),
        out_shape=jax.ShapeDtypeStruct((bsz, t, d_ff), BF16),
        scratch_shapes=[pltpu.VMEM((tm + 2 * halo, d), BF16)],
        compiler_params=_params(("arbitrary", "arbitrary", "arbitrary")),
        name="ffn_up_conv",
    )(x, x, x, gain.reshape(1, d), shift, scale, w_up, w_up, conv_w, conv_w, conv_b, conv_b)


def _attn_a_kernel(q_ref, k_ref, v_ref, o_ref, m_ref, acc_ref, *, tq, rs):
    step = pl.program_id(3)

    @pl.when(step == 0)
    def _():
        m_ref[...] = jnp.full(m_ref.shape, -jnp.inf, F32)
        acc_ref[...] = jnp.zeros(acc_ref.shape, F32)

    k = k_ref[...]
    v = v_ref[...]
    n_chunk = k.shape[0] // LANES
    chains = [(g, r) for g in range(GROUP) for r in range(tq // rs)]
    for start in range(0, len(chains), CHAINS_PER_STAGE):
        group = chains[start:start + CHAINS_PER_STAGE]
        rows = [slice(g * tq + r * rs, g * tq + (r + 1) * rs) for g, r in group]
        scores = [_dot_nt(q_ref[r * rs:(r + 1) * rs, g * HEAD_DIM:(g + 1) * HEAD_DIM], k) for g, r in group]
        chunks = [[s[:, c * LANES:(c + 1) * LANES] for c in range(n_chunk)] for s in scores]
        m_prev = [m_ref[rw, :] for rw in rows]
        m_new = [jnp.maximum(mp, jnp.max(functools.reduce(jnp.maximum, ch), axis=-1, keepdims=True))
                 for mp, ch in zip(m_prev, chunks)]
        probs = [jnp.concatenate([jnp.exp2(c - mn) for c in ch], axis=1).astype(BF16) for mn, ch in zip(m_new, chunks)]
        pv = [_dot(p, v) for p in probs]
        for rw, mp, mn, upd in zip(rows, m_prev, m_new, pv):
            alpha = jnp.exp2(mp - mn)
            acc_ref[rw, :] = jnp.concatenate([alpha, alpha], axis=1) * acc_ref[rw, :] + upd
            m_ref[rw, :] = mn

    @pl.when(step == pl.num_programs(3) - 1)
    def _():
        for g in range(GROUP):
            acc = acc_ref[g * tq:(g + 1) * tq, :]
            o_ref[:, g * HEAD_DIM:(g + 1) * HEAD_DIM] = (acc[:, :HEAD_DIM] / acc[:, HEAD_DIM:]).astype(o_ref.dtype)


def _attn_a(qk, k_all, v_ext, tq_pref=1024, tk_pref=3328, rs_pref=256):
    bsz, t, _ = qk.shape
    s = k_all.shape[1]
    tq = _pick(t, tq_pref, BF16_ROWS)
    rs = _pick(tq, rs_pref, BF16_ROWS)
    tk = _pick(s, tk_pref, LANES)
    gw = GROUP * HEAD_DIM
    m = GROUP * tq
    return pl.pallas_call(
        functools.partial(_attn_a_kernel, tq=tq, rs=rs),
        grid=(bsz, N_KV_HEADS, t // tq, s // tk),
        in_specs=[pl.BlockSpec((None, tq, gw), lambda b, h, i, j: (b, i, h)),
                  pl.BlockSpec((None, tk, HEAD_DIM), lambda b, h, i, j: (b, j, h)),
                  pl.BlockSpec((None, tk, 2 * HEAD_DIM), lambda b, h, i, j: (b, j, h))],
        out_specs=pl.BlockSpec((None, tq, gw), lambda b, h, i, j: (b, i, h)),
        out_shape=jax.ShapeDtypeStruct((bsz, t, N_KV_HEADS * gw), BF16),
        scratch_shapes=[pltpu.VMEM((m, HEAD_DIM), F32), pltpu.VMEM((m, 2 * HEAD_DIM), F32)],
        compiler_params=_params(("arbitrary", "arbitrary", "arbitrary", "arbitrary")),
        name="attn_global",
    )(qk, k_all, v_ext)


def _softmax_out(s, sink, v):
    sink = sink * LOG2E
    m = jnp.maximum(jnp.max(s, axis=-1, keepdims=True), sink)
    p = jnp.exp2(s - m)
    den = jnp.sum(p, axis=-1, keepdims=True) + jnp.exp2(sink - m)
    return _dot(p.astype(BF16), v) / den


def _attn_b_kernel(sink_ref, bias_ref, q_ref, kc_ref, vc_ref, kp_ref, km_ref, kn_ref, vp_ref, vm_ref, vn_ref,
                   o_ref, kcat_ref, vcat_ref):
    kv = pl.program_id(1)
    for dst, parts in ((kcat_ref, (kc_ref, kp_ref, km_ref, kn_ref)), (vcat_ref, (vc_ref, vp_ref, vm_ref, vn_ref))):
        off = 0
        for part in parts:
            dst[off:off + part.shape[0], :] = part[...]
            off += part.shape[0]
    kcat = kcat_ref[...]
    vcat = vcat_ref[...]
    bias = bias_ref[...]
    heads = range(GROUP)
    cols = [slice(g * HEAD_DIM, (g + 1) * HEAD_DIM) for g in heads]
    scores = [_dot_nt(q_ref[:, cols[g]], kcat) + bias for g in heads]
    sinks = [sink_ref[kv * GROUP + g] * LOG2E for g in heads]
    tops = [jnp.maximum(jnp.max(scores[g], axis=-1, keepdims=True), sinks[g]) for g in heads]
    probs = [jnp.exp2(scores[g] - tops[g]) for g in heads]
    dens = [jnp.sum(probs[g], axis=-1, keepdims=True) + jnp.exp2(sinks[g] - tops[g]) for g in heads]
    outs = [_dot(probs[g].astype(BF16), vcat) for g in heads]
    for g in heads:
        o_ref[:, cols[g]] = (outs[g] / dens[g]).astype(o_ref.dtype)


def _attn_b(qk, v, qk_c, v_c, sink, tq_pref=256):
    bsz, t, _ = qk.shape
    n_ctx = qk_c.shape[1]
    tq = _pick(t, tq_pref, WINDOW)
    r = tq // WINDOW
    n_w = t // WINDOW
    gw = GROUP * HEAD_DIM
    kb, vb = 18, 2
    prev = lambda c: (lambda b, h, i: (b, jnp.maximum(i * r - 1, 0), c + h))
    main = lambda c: (lambda b, h, i: (b, i, c + h))
    nxt = lambda c: (lambda b, h, i: (b, jnp.minimum((i + 1) * r, n_w - 1), c + h))
    nk = n_ctx + tq + 2 * WINDOW
    n_i = t // tq
    row = np.arange(tq)[:, None]
    col = np.arange(nk)[None, :]
    band = (col < n_ctx) | (np.abs(col - n_ctx - WINDOW - row) <= WINDOW)
    before = (col >= n_ctx) & (col < n_ctx + WINDOW)
    after = col >= n_ctx + WINDOW + tq
    bias = np.stack([np.where(band & ~(before & bool(v & 1)) & ~(after & bool(v & 2)), 0.0, NEG_INF)
                     for v in range(4)]).astype(np.float32)
    variant = lambda i: jnp.where(i == 0, 1, 0) + jnp.where(i == n_i - 1, 2, 0)
    return pl.pallas_call(
        _attn_b_kernel,
        grid=(bsz, N_KV_HEADS, n_i),
        in_specs=[pl.BlockSpec(memory_space=pltpu.SMEM),
                  pl.BlockSpec((None, tq, nk), lambda b, h, i: (variant(i), 0, 0)),
                  pl.BlockSpec((None, tq, gw), lambda b, h, i: (b, i, 2 + h)),
                  pl.BlockSpec((None, n_ctx, HEAD_DIM), lambda b, h, i: (b, 0, kb + h)),
                  pl.BlockSpec((None, n_ctx, HEAD_DIM), lambda b, h, i: (b, 0, vb + h)),
                  pl.BlockSpec((None, WINDOW, HEAD_DIM), prev(kb)),
                  pl.BlockSpec((None, tq, HEAD_DIM), main(kb)),
                  pl.BlockSpec((None, WINDOW, HEAD_DIM), nxt(kb)),
                  pl.BlockSpec((None, WINDOW, HEAD_DIM), prev(vb)),
                  pl.BlockSpec((None, tq, HEAD_DIM), main(vb)),
                  pl.BlockSpec((None, WINDOW, HEAD_DIM), nxt(vb))],
        out_specs=pl.BlockSpec((None, tq, gw), lambda b, h, i: (b, i, h)),
        out_shape=jax.ShapeDtypeStruct((bsz, t, N_KV_HEADS * gw), BF16),
        scratch_shapes=[pltpu.VMEM((nk, HEAD_DIM), BF16), pltpu.VMEM((nk, HEAD_DIM), BF16)],
        compiler_params=_params(("arbitrary", "arbitrary", "arbitrary")),
        name="attn_window",
    )(sink, jnp.asarray(bias), qk, qk_c, v_c, qk, qk, qk, v, v, v)


def _attn_ctx_kernel(sink_ref, q_ref, k_ref, v_ref, o_ref):
    grp = pl.program_id(1)
    k = k_ref[...]
    v = v_ref[...]
    for g in range(GROUP):
        cols = slice(g * HEAD_DIM, (g + 1) * HEAD_DIM)
        s = _dot_nt(q_ref[:, cols], k)
        o_ref[:, cols] = _softmax_out(s, sink_ref[grp * GROUP + g], v).astype(o_ref.dtype)


def _attn_ctx(qk_c, v_c, sink_all):
    bsz, n_ctx, _ = qk_c.shape
    gw = GROUP * HEAD_DIM
    n_grp = 2 * N_KV_HEADS
    return pl.pallas_call(
        _attn_ctx_kernel,
        grid=(bsz, n_grp),
        in_specs=[pl.BlockSpec(memory_space=pltpu.SMEM),
                  pl.BlockSpec((None, n_ctx, gw), lambda b, h: (b, 0, h)),
                  pl.BlockSpec((None, n_ctx, HEAD_DIM), lambda b, h: (b, 0, 16 + h)),
                  pl.BlockSpec((None, n_ctx, HEAD_DIM), lambda b, h: (b, 0, h))],
        out_specs=pl.BlockSpec((None, n_ctx, gw), lambda b, h: (b, 0, h)),
        out_shape=jax.ShapeDtypeStruct((bsz, n_ctx, n_grp * gw), BF16),
        compiler_params=_params(("arbitrary", "arbitrary")),
        name="attn_ctx",
    )(sink_all, qk_c, qk_c, v_c)


def _scan_consts(chunk, reverse, base):
    idx = np.arange(chunk)
    t = idx[:, None]
    s = idx[None, :]
    tri = ((s >= t) if reverse else (s <= t)).astype(np.float32)
    blocks = [tri]
    masks = []
    h = chunk // 2
    while h >= base:
        same = (t // (2 * h)) == (s // (2 * h))
        t_hi = (t // h) % 2 == 1
        s_hi = (s // h) % 2 == 1
        masks.append(same & (~t_hi & s_hi if reverse else t_hi & ~s_hi))
        ref = (idx // (2 * h)) * (2 * h) + (h if reverse else h - 1)
        blocks.append(tri - tri[ref, :])
        h //= 2
    same = (t // base) == (s // base)
    masks.append(same & ((s >= t) if reverse else (s <= t)))
    if base > 1:
        blocks.append(tri * same)
    dist = np.concatenate(blocks, axis=0)
    return np.stack(masks).astype(np.float32), np.concatenate([dist, dist], axis=1)


def _block_prep(order, rows, dist, reverse, base, k_ref, lf_ref, v_ref, q_ref=None, masks=None):
    chunk = rows(order[0]).stop - rows(order[0]).start
    n_level = (chunk // base).bit_length() - 1
    dists = {}
    for pos in range(0, len(order), 2):
        pair = order[pos:pos + 2]
        cols = []
        for c in pair:
            lf = lf_ref[rows(c), :]
            hi = lf.astype(BF16)
            cols.append(jnp.concatenate([hi, (lf - hi.astype(F32)).astype(BF16)], axis=0))
        dd = _dot(dist, cols[0] if len(cols) == 1 else jnp.concatenate(cols, axis=1))
        for i, c in enumerate(pair):
            dists[c] = dd[:, i * REC_DK:(i + 1) * REC_DK]
    preps = {}
    for c in order:
        d = dists[c]
        cum = d[0:chunk]
        total = cum[0:1, :] if reverse else cum[chunk - 1:chunk, :]
        k = k_ref[rows(c), :]
        prep = dict(decay=jnp.exp2(total), k_out=k * jnp.exp2(total - cum).astype(BF16))
        if q_ref is not None:
            q = q_ref[rows(c), :]
            prep["q_in"] = q * jnp.exp2(cum).astype(BF16)
            pairs = []
            for level in range(n_level):
                e = jnp.exp2(-jnp.abs(d[(level + 1) * chunk:(level + 2) * chunk])).astype(BF16)
                pairs.append((q * e, k * e))
            if base == 1:
                pairs.append((q, k))
            else:
                since = d[(n_level + 1) * chunk:(n_level + 2) * chunk]
                pairs.append((q * jnp.exp2(since).astype(BF16), k * jnp.exp2(-since).astype(BF16)))
            prep["pairs"] = pairs
        preps[c] = prep
    for c in order:
        prep = preps[c]
        prep["kv"] = _dot(v_ref[rows(c), :].astype(F32).T.astype(BF16), prep.pop("k_out"))
        if q_ref is not None:
            prep["terms"] = [_dot_nt(qe, ke) for qe, ke in prep.pop("pairs")]
    if q_ref is not None:
        for c in order:
            terms = preps[c].pop("terms")
            scores = jnp.where(masks[n_level] > 0.0, terms[n_level], 0.0)
            for level in range(n_level):
                scores = scores + masks[level] * terms[level]
            preps[c]["scores"] = scores.astype(BF16)
        for c in order:
            preps[c]["intra"] = _dot(preps[c].pop("scores"), v_ref[rows(c), :])
    return preps


def _chunk_apply(st, prep):
    st_new = st * prep["decay"] + prep["kv"]
    if "intra" not in prep:
        return st_new, None
    return st_new, _dot_nt(prep["q_in"], st.astype(BF16)) + prep["intra"]


def _scan_order(n, reverse):
    return list(range(n - 1, -1, -1) if reverse else range(n))


def _scan_state_kernel(k_ref, lf_ref, v_ref, s0_ref, dist_ref, sfin_ref, st_ref, *, chunk, reverse):
    step = pl.program_id(2)

    @pl.when(step == 0)
    def _():
        st_ref[...] = s0_ref[...]

    rows = lambda c: slice(c * chunk, (c + 1) * chunk)
    order = _scan_order(k_ref.shape[0] // chunk, reverse)
    preps = _block_prep(order, rows, dist_ref[...], reverse, 1, k_ref, lf_ref, v_ref)
    st = st_ref[...]
    for c in order:
        st, _ = _chunk_apply(st, preps[c])
    st_ref[...] = st

    @pl.when(step == pl.num_programs(2) - 1)
    def _():
        sfin_ref[...] = st


def _scan_out_kernel(*refs, chunk, reverse, final, base):
    if final:
        q_ref, k_ref, lf_ref, v_ref, s0_ref, dist_ref, masks_ref, of_ref, g_ref, gain_ref, o_ref, st_ref = refs
    else:
        q_ref, k_ref, lf_ref, v_ref, s0_ref, dist_ref, masks_ref, o_ref, st_ref = refs

    @pl.when(pl.program_id(2) == 0)
    def _():
        st_ref[...] = s0_ref[...]

    rows = lambda c: slice(c * chunk, (c + 1) * chunk)
    order = _scan_order(k_ref.shape[0] // chunk, reverse)
    preps = _block_prep(order, rows, dist_ref[...], reverse, base, k_ref, lf_ref, v_ref, q_ref, masks_ref[...])
    st = st_ref[...]
    for c in order:
        st, out = _chunk_apply(st, preps[c])
        if final:
            tot = out + of_ref[rows(c), :]
            ms = jnp.mean(tot * tot, axis=-1, keepdims=True)
            gate = g_ref[rows(c), :].astype(F32)
            out = tot * lax.rsqrt(ms + EPS) * gain_ref[...] * (gate * jax.nn.sigmoid(gate))
        o_ref[rows(c), :] = out.astype(o_ref.dtype)
    st_ref[...] = st


def _scan_geometry(t, blk_pref=4096, chunk_pref=128):
    chunk = _pick(t, chunk_pref, LANES)
    blk = _pick(t, blk_pref, chunk)
    return chunk, blk


def _scan_state(kf, lf, qig, s0, direction):
    bsz, t, _ = kf.shape
    n_h = s0.shape[1]
    reverse = direction == 1
    chunk, blk = _scan_geometry(t)
    n_blk = t // blk
    dist = _scan_consts(chunk, reverse, 1)[1][:chunk]
    pos = (lambda c: n_blk - 1 - c) if reverse else (lambda c: c)
    kcol = direction * n_h
    return pl.pallas_call(
        functools.partial(_scan_state_kernel, chunk=chunk, reverse=reverse),
        grid=(bsz, n_h, n_blk),
        in_specs=[pl.BlockSpec((None, blk, REC_DK), lambda b, h, c: (b, pos(c), kcol + h)),
                  pl.BlockSpec((None, blk, REC_DK), lambda b, h, c: (b, pos(c), kcol + h)),
                  pl.BlockSpec((None, blk, REC_DK), lambda b, h, c: (b, pos(c), n_h + h)),
                  pl.BlockSpec((None, None, REC_DK, REC_DK), lambda b, h, c: (b, h, 0, 0)),
                  pl.BlockSpec(dist.shape, lambda b, h, c: (0, 0))],
        out_specs=pl.BlockSpec((None, None, REC_DK, REC_DK), lambda b, h, c: (b, h, 0, 0)),
        out_shape=jax.ShapeDtypeStruct(s0.shape, F32),
        scratch_shapes=[pltpu.VMEM((REC_DK, REC_DK), F32)],
        compiler_params=_params(("arbitrary", "arbitrary", "arbitrary")),
        name="hgrn_state_bwd" if reverse else "hgrn_state_fwd",
    )(kf, lf, qig, s0, jnp.asarray(dist, BF16))


def _scan_out(kf, lf, qig, s0, direction, base, other=None, out_gain=None):
    bsz, t, _ = kf.shape
    n_h = s0.shape[1]
    reverse = direction == 1
    final = other is not None
    chunk, blk = _scan_geometry(t)
    n_blk = t // blk
    masks, dist = _scan_consts(chunk, reverse, base)
    pos = (lambda c: n_blk - 1 - c) if reverse else (lambda c: c)
    kcol = direction * n_h
    tok = lambda col: pl.BlockSpec((None, blk, REC_DK), lambda b, h, c: (b, pos(c), col + h))
    in_specs = [tok(0), tok(kcol), tok(kcol), tok(n_h),
                pl.BlockSpec((None, None, REC_DK, REC_DK), lambda b, h, c: (b, h, 0, 0)),
                pl.BlockSpec(dist.shape, lambda b, h, c: (0, 0)),
                pl.BlockSpec(masks.shape, lambda b, h, c: (0, 0, 0))]
    args = [qig, kf, lf, qig, s0, jnp.asarray(dist, BF16), jnp.asarray(masks)]
    if final:
        in_specs += [tok(0), tok(2 * n_h), pl.BlockSpec((1, REC_DK), lambda b, h, c: (0, 0))]
        args += [other, qig, out_gain.reshape(1, REC_DK)]
    return pl.pallas_call(
        functools.partial(_scan_out_kernel, chunk=chunk, reverse=reverse, final=final, base=base),
        grid=(bsz, n_h, n_blk),
        in_specs=in_specs,
        out_specs=tok(0),
        out_shape=jax.ShapeDtypeStruct((bsz, t, n_h * REC_DK), BF16 if final else F32),
        scratch_shapes=[pltpu.VMEM((REC_DK, REC_DK), F32)],
        compiler_params=_params(("arbitrary", "arbitrary", "arbitrary")),
        name="hgrn_scan_bwd" if reverse else "hgrn_scan_fwd",
    )(*args)


def _rope_tables(n_tokens):
    n_rows = n_tokens // GRID_W
    quarter = HEAD_DIM // 4
    inv_freq = ROPE_THETA ** (-jnp.arange(quarter, dtype=F32) / quarter)
    row_ang = jnp.arange(n_rows, dtype=F32)[:, None] * inv_freq
    col_ang = jnp.arange(GRID_W, dtype=F32)[:, None] * inv_freq

    def table(fn):
        by_row = jnp.broadcast_to(fn(row_ang)[:, None, :], (n_rows, GRID_W, quarter))
        by_col = jnp.broadcast_to(fn(col_ang)[None, :, :], (n_rows, GRID_W, quarter))
        return jnp.concatenate([by_row, by_col], axis=-1).reshape(n_tokens, 2 * quarter)

    cos, sin = table(jnp.cos), table(jnp.sin)
    return jnp.concatenate([cos, cos], axis=-1), jnp.concatenate([-sin, sin], axis=-1)


def _conv_ffn(x, gain, shift, scale, gate, w_up, conv_w, conv_b, w_down):
    act = _ffn_up(x, gain, shift, scale, w_up, conv_w, conv_b)
    return _mm_res([act], [w_down], x, gate)


def kernel(x, c, ctx, c_ctx, w_mod, b_mod, norm_gain, attn_w_in, attn_w_out, attn_qk_gain, attn_sink,
           rec_w_in, rec_w_out, rec_out_gain, rec_lb_gamma, ffn_w_up, ffn_conv_w, ffn_conv_b, ffn_w_down):
    bsz, n_lat, d = x.shape
    depth = w_mod.shape[0]
    assert depth == 2 and bsz < SUBLANES, "layer 0 attention, layer 1 recurrence; conditioning rows fit one tile"
    n_heads_rec = d // REC_DK

    cvec = jnp.zeros((SUBLANES, d), F32).at[:bsz].set(c).at[bsz].set(c_ctx)
    mods = _mods(cvec, w_mod, b_mod)

    def lat_mod(layer, k):
        return mods[layer, :bsz, k * d:(k + 1) * d][:, None, :]

    def ctx_mod(layer, k):
        return jnp.broadcast_to(mods[layer, bsz, k * d:(k + 1) * d][None, None, :], (bsz, 1, d))

    w_up = ffn_w_up.astype(BF16)
    w_down = ffn_w_down.astype(BF16)

    aq = N_Q_HEADS * HEAD_DIM
    akv = N_KV_HEADS * HEAD_DIM
    w_in = attn_w_in[0]
    qa, ka, va, qb, kb, vb = jnp.split(w_in, np.cumsum([aq, akv, akv, aq, akv])[:5].tolist(), axis=1)
    w_qk = jnp.concatenate([qa, qb, ka, kb], axis=1)
    w_v = jnp.concatenate([va, vb], axis=1)
    qkg = attn_qk_gain[0]
    q_scale = ATTN_SCALE * LOG2E
    head_gain = jnp.concatenate([jnp.tile(qkg[0] * q_scale, N_Q_HEADS), jnp.tile(qkg[2] * q_scale, N_Q_HEADS),
                                 jnp.tile(qkg[1], N_KV_HEADS), jnp.tile(qkg[3], N_KV_HEADS)])[None, :]
    cos, sin = _rope_tables(n_lat)
    g0 = norm_gain[0, 0]
    qk_l, v_l = _proj_qkv(x, g0, lat_mod(0, 0), lat_mod(0, 1), w_qk, w_v, head_gain, cos, sin)
    qk_c, v_c = _proj_qkv(ctx, g0, ctx_mod(0, 0), ctx_mod(0, 1), w_qk, w_v, head_gain)

    ka_cols = slice(2 * aq, 2 * aq + akv)
    k_all = jnp.concatenate([qk_c[:, :, ka_cols], qk_l[:, :, ka_cols]], axis=1)
    v_all = jnp.concatenate([v_c[:, :, :akv], v_l[:, :, :akv]], axis=1)
    ones = jnp.ones(v_all.shape[:2] + (HEAD_DIM,), BF16)
    v_ext = jnp.concatenate([v_all[:, :, :HEAD_DIM], ones, v_all[:, :, HEAD_DIM:], ones], axis=-1)
    sink = attn_sink[0].astype(F32)
    o_a = _attn_a(qk_l, k_all, v_ext)
    o_b = _attn_b(qk_l, v_l, qk_c, v_c, sink)
    o_c = _attn_ctx(qk_c, v_c, jnp.concatenate([jnp.full((N_Q_HEADS,), -jnp.inf, F32), sink]))

    w_out = attn_w_out[0].astype(BF16)
    x = _mm_res([o_a, o_b], [w_out[:aq], w_out[aq:]], x, lat_mod(0, 2))
    ctx = _mm_res([o_c], [w_out], ctx, ctx_mod(0, 2))
    g1 = norm_gain[0, 1]
    x = _conv_ffn(x, g1, lat_mod(0, 3), lat_mod(0, 4), lat_mod(0, 5), w_up[0], ffn_conv_w[0], ffn_conv_b[0], w_down[0])
    ctx = _conv_ffn(ctx, g1, ctx_mod(0, 3), ctx_mod(0, 4), ctx_mod(0, 5), w_up[0], ffn_conv_w[0], ffn_conv_b[0],
                    w_down[0])

    lb_soft = jax.nn.softmax(rec_lb_gamma.astype(F32), axis=1)
    lower = (jnp.cumsum(lb_soft, axis=1) - lb_soft[:, :1])[:, 1]
    rk = n_heads_rec * REC_DK
    w_q, w_ff, w_fb, w_i, w_g = jnp.split(rec_w_in[0], [rk, 2 * rk, 3 * rk, 3 * rk + d], axis=1)
    w_qig = jnp.concatenate([w_q, w_i, w_g], axis=1)
    w_f = jnp.concatenate([w_ff, w_fb], axis=1)
    lb = lower.reshape(1, 2 * rk)
    g0 = norm_gain[1, 0]
    lf_l, kf_l, qig_l = _proj_rec(x, g0, lat_mod(1, 0), lat_mod(1, 1), w_f, w_qig, lb)
    lf_c, kf_c, qig_c = _proj_rec(ctx, g0, ctx_mod(1, 0), ctx_mod(1, 1), w_f, w_qig, lb)

    s_zero = jnp.zeros((bsz, n_heads_rec, REC_DK, REC_DK), F32)
    s_fwd = _scan_state(kf_c, lf_c, qig_c, s_zero, 0)
    s_bwd = _scan_state(kf_c, lf_c, qig_c, s_zero, 1)

    def scans(base):
        def run(kf, lf, qig, s_f, s_b, gain):
            o_f = _scan_out(kf, lf, qig, s_f, 0, base)
            return _scan_out(kf, lf, qig, s_b, 1, base, other=o_f, out_gain=gain)
        return run

    worst_exp2 = FAST_BASE * jnp.max(-jnp.log2(lower))
    y = lax.cond(worst_exp2 <= MAX_EXP2, scans(FAST_BASE), scans(1),
                 kf_l, lf_l, qig_l, s_fwd, s_bwd, rec_out_gain[0])
    x = _mm_res([y], [rec_w_out[0].astype(BF16)], x, lat_mod(1, 2))
    g1 = norm_gain[1, 1]
    x = _conv_ffn(x, g1, lat_mod(1, 3), lat_mod(1, 4), lat_mod(1, 5), w_up[1], ffn_conv_w[1], ffn_conv_b[1], w_down[1])
    return x
```
